```python
import jax, jax.numpy as jnp
from jax import lax
import numpy as np

D_MODEL = 2048
BATCH = 4
SEQ = 4096
DEPTH = 2

MEM_LEN = 256
EPS = 1e-6
HEAD_DIM = 64
N_Q_HEADS = 16
N_KV_HEADS = 2
Q_PER_KV = N_Q_HEADS // N_KV_HEADS
ATTN_WIDTH = N_Q_HEADS * HEAD_DIM
KV_WIDTH = N_KV_HEADS * HEAD_DIM
WINDOW = 128
BLOCK = 128
ROPE_DIM = HEAD_DIM // 4
ROPE_THETA = 500000.0
SGU_GROUPS = 8
SGU_WIDTH = D_MODEL // 2
SGU_GROUP_DIM = SGU_WIDTH // SGU_GROUPS
CHUNK = 128
IN_WIDTH = ATTN_WIDTH + 2 * KV_WIDTH + 2 * SGU_WIDTH
MIX_WIDTH = ATTN_WIDTH + SGU_WIDTH
POOL_WINDOWS = (2, 4, 8, 16)
N_POOL_GROUPS = len(POOL_WINDOWS)
POOL_GROUP_DIM = D_MODEL // N_POOL_GROUPS
X_HEADS = 4
X_HEAD_DIM = 128
X_WIDTH = X_HEADS * X_HEAD_DIM
D_FF = 5632
N_NORMS = 8
N_EVEN = (DEPTH + 1) // 2
N_ODD = DEPTH // 2

kernel_name = "hybrid_swa_sgu_pool_macaron"


def rms_norm(x, g):
    xf = x.astype(jnp.float32)
    y = xf * lax.rsqrt(jnp.mean(xf * xf, axis=-1, keepdims=True) + EPS)
    return (y * g.astype(jnp.float32)).astype(x.dtype)


def swiglu(h, wg, wu, wd):
    return (jax.nn.silu(h @ wg) * (h @ wu)) @ wd


def rope_tables(seq):
    half = ROPE_DIM // 2
    inv = ROPE_THETA ** (-jnp.arange(half, dtype=jnp.float32) * 2.0 / ROPE_DIM)
    ang = jnp.arange(seq, dtype=jnp.float32)[:, None] * inv[None, :]
    return jnp.cos(ang)[:, None, :], jnp.sin(ang)[:, None, :]


def partial_rope(x, cos, sin):
    xf = x.astype(jnp.float32)
    half = ROPE_DIM // 2
    x1 = xf[..., :half]
    x2 = xf[..., half:ROPE_DIM]
    rot = jnp.concatenate([x1 * cos - x2 * sin, x2 * cos + x1 * sin, xf[..., ROPE_DIM:]], axis=-1)
    return rot.astype(x.dtype)


def swa_sink_attention(q, k, v, sinks):
    b, s = q.shape[0], q.shape[1]
    nb = s // BLOCK
    qb = q.reshape(b, nb, BLOCK, N_KV_HEADS, Q_PER_KV, HEAD_DIM)
    pad = ((0, 0), (BLOCK, 0), (0, 0), (0, 0))
    kp = jnp.pad(k, pad).reshape(b, nb + 1, BLOCK, N_KV_HEADS, HEAD_DIM)
    vp = jnp.pad(v, pad).reshape(b, nb + 1, BLOCK, N_KV_HEADS, HEAD_DIM)
    kb = jnp.concatenate([kp[:, :-1], kp[:, 1:]], axis=2)
    vb = jnp.concatenate([vp[:, :-1], vp[:, 1:]], axis=2)
    scores = jnp.einsum('bnqhgd,bnkhd->bnhgqk', qb, kb,
                        preferred_element_type=jnp.float32) * (HEAD_DIM ** -0.5)
    qi = jnp.arange(BLOCK)[:, None]
    kj = jnp.arange(2 * BLOCK)[None, :]
    rel = qi + BLOCK - kj
    band = (rel >= 0) & (rel < WINDOW)
    not_pad = (jnp.arange(nb)[:, None, None] > 0) | (kj >= BLOCK)[None]
    valid = band[None] & not_pad
    scores = jnp.where(valid[None, :, None, None], scores, jnp.float32(-1e30))
    sink = jnp.broadcast_to(
        sinks.astype(jnp.float32).reshape(N_KV_HEADS, Q_PER_KV)[None, None, :, :, None, None],
        scores.shape[:-1] + (1,))
    probs = jax.nn.softmax(jnp.concatenate([scores, sink], axis=-1), axis=-1)[..., :-1]
    out = jnp.einsum('bnhgqk,bnkhd->bnqhgd', probs.astype(v.dtype), vb)
    return out.reshape(b, s, ATTN_WIDTH)


def chunked_spatial_gating(u, v, ln_g, ln_b, w_s, b_s):
    b, s = u.shape[0], u.shape[1]
    nc = s // CHUNK
    vf = v.astype(jnp.float32)
    mu = jnp.mean(vf, axis=-1, keepdims=True)
    var = jnp.mean(jnp.square(vf - mu), axis=-1, keepdims=True)
    vn = ((vf - mu) * lax.rsqrt(var + EPS) * ln_g.astype(jnp.float32) + ln_b.astype(jnp.float32)).astype(v.dtype)
    vc = vn.reshape(b, nc, CHUNK, SGU_GROUPS, SGU_GROUP_DIM)
    causal = jnp.tril(jnp.ones((CHUNK, CHUNK), dtype=w_s.dtype))
    mixed = jnp.einsum('gij,bnjgc->bnigc', w_s * causal[None], vc) \
        + jnp.transpose(b_s)[None, None, :, :, None]
    return u * mixed.reshape(b, s, SGU_WIDTH).astype(u.dtype)


def attn_sgu_mixer(h, w_in, w_out, sinks, ln_g, ln_b, w_s, b_s, cos, sin):
    b, s = h.shape[0], h.shape[1]
    z = h @ w_in
    o1 = ATTN_WIDTH
    o2 = o1 + KV_WIDTH
    o3 = o2 + KV_WIDTH
    o4 = o3 + SGU_WIDTH
    q = partial_rope(z[..., :o1].reshape(b, s, N_Q_HEADS, HEAD_DIM), cos, sin)
    k = partial_rope(z[..., o1:o2].reshape(b, s, N_KV_HEADS, HEAD_DIM), cos, sin)
    v = z[..., o2:o3].reshape(b, s, N_KV_HEADS, HEAD_DIM)
    attn = swa_sink_attention(q, k, v, sinks)
    gate = chunked_spatial_gating(jax.nn.gelu(z[..., o3:o4]), jax.nn.gelu(z[..., o4:]), ln_g, ln_b, w_s, b_s)
    return jnp.concatenate([attn, gate], axis=-1) @ w_out


def multiscale_pool_mixer(h, pool_w, pool_scale):
    b, s = h.shape[0], h.shape[1]
    hf = h.astype(jnp.float32).reshape(b, s, N_POOL_GROUPS, POOL_GROUP_DIM)
    cs = jnp.cumsum(hf, axis=1)
    count = jnp.arange(1, s + 1, dtype=jnp.float32)
    outs = []
    for gi, w in enumerate(POOL_WINDOWS):
        c = cs[:, :, gi]
        prev = jnp.pad(c, ((0, 0), (w, 0), (0, 0)))[:, :s]
        mean = (c - prev) / jnp.minimum(count, jnp.float32(w))[None, :, None]
        outs.append(mean - hf[:, :, gi])
    pooled = jnp.stack(outs, axis=2).astype(h.dtype)
    y = jnp.einsum('bsgc,gcd->bsgd', pooled, pool_w).reshape(b, s, D_MODEL)
    return y * pool_scale


def memory_cross_attention(h, mem_n, wq, wk, wv, wo):
    b, s = h.shape[0], h.shape[1]
    m = mem_n.shape[1]
    q = (h @ wq).reshape(b, s, X_HEADS, X_HEAD_DIM)
    k = (mem_n @ wk).reshape(b, m, X_HEADS, X_HEAD_DIM)
    v = (mem_n @ wv).reshape(b, m, X_HEADS, X_HEAD_DIM)
    sc = jnp.einsum('bshd,bmhd->bhsm', q, k, preferred_element_type=jnp.float32) * (X_HEAD_DIM ** -0.5)
    p = jax.nn.softmax(sc, axis=-1)
    o = jnp.einsum('bhsm,bmhd->bshd', p.astype(v.dtype), v).reshape(b, s, X_WIDTH)
    return o @ wo


def setup_inputs(seed: int = 0) -> dict:
    key = jax.random.key(seed)
    ks = jax.random.split(key, 24)
    f32 = jnp.float32

    def w(k, shape, fan_in):
        return jax.random.normal(k, shape, f32) * (fan_in ** -0.5)

    return {
        "x": jax.random.normal(ks[0], (BATCH, SEQ, D_MODEL), f32),
        "mem": jax.random.normal(ks[1], (BATCH, MEM_LEN, D_MODEL), f32),
        "norms": 1.0 + 0.1 * jax.random.normal(ks[2], (DEPTH, N_NORMS, D_MODEL), f32),
        "mem_norm": 1.0 + 0.1 * jax.random.normal(ks[3], (DEPTH, D_MODEL), f32),
        "ffn1_wg": w(ks[4], (DEPTH, D_MODEL, D_FF), D_MODEL),
        "ffn1_wu": w(ks[5], (DEPTH, D_MODEL, D_FF), D_MODEL),
        "ffn1_wd": w(ks[6], (DEPTH, D_FF, D_MODEL), D_FF),
        "ffn2_wg": w(ks[7], (DEPTH, D_MODEL, D_FF), D_MODEL),
        "ffn2_wu": w(ks[8], (DEPTH, D_MODEL, D_FF), D_MODEL),
        "ffn2_wd": w(ks[9], (DEPTH, D_FF, D_MODEL), D_FF),
        "x_wq": w(ks[10], (DEPTH, D_MODEL, X_WIDTH), D_MODEL),
        "x_wk": w(ks[11], (DEPTH, D_MODEL, X_WIDTH), D_MODEL),
        "x_wv": w(ks[12], (DEPTH, D_MODEL, X_WIDTH), D_MODEL),
        "x_wo": w(ks[13], (DEPTH, X_WIDTH, D_MODEL), X_WIDTH),
        "mix_w_in": w(ks[14], (N_EVEN, D_MODEL, IN_WIDTH), D_MODEL),
        "mix_w_out": w(ks[15], (N_EVEN, MIX_WIDTH, D_MODEL), MIX_WIDTH),
        "attn_sinks": 0.5 * jax.random.normal(ks[16], (N_EVEN, N_Q_HEADS), f32),
        "sgu_ln_g": 1.0 + 0.1 * jax.random.normal(ks[17], (N_EVEN, SGU_WIDTH), f32),
        "sgu_ln_b": 0.02 * jax.random.normal(ks[18], (N_EVEN, SGU_WIDTH), f32),
        "sgu_w": w(ks[19], (N_EVEN, SGU_GROUPS, CHUNK, CHUNK), CHUNK),
        "sgu_b": 1.0 + 0.1 * jax.random.normal(ks[20], (N_EVEN, SGU_GROUPS, CHUNK), f32),
        "pool_w": w(ks[21], (N_ODD, N_POOL_GROUPS, POOL_GROUP_DIM, POOL_GROUP_DIM), POOL_GROUP_DIM),
        "pool_scale": 1.0 + 0.2 * jax.random.normal(ks[22], (N_ODD, D_MODEL), f32),
    }


def reference(x, mem, norms, mem_norm, ffn1_wg, ffn1_wu, ffn1_wd, ffn2_wg, ffn2_wu, ffn2_wd,
              x_wq, x_wk, x_wv, x_wo, mix_w_in, mix_w_out, attn_sinks, sgu_ln_g, sgu_ln_b,
              sgu_w, sgu_b, pool_w, pool_scale):
    cos, sin = rope_tables(x.shape[1])
    for layer in range(DEPTH):
        g = norms[layer]
        h = rms_norm(x, g[0])
        x = x + 0.5 * rms_norm(swiglu(h, ffn1_wg[layer], ffn1_wu[layer], ffn1_wd[layer]), g[1])
        h = rms_norm(x, g[2])
        i = layer // 2
        if layer % 2 == 0:
            m = attn_sgu_mixer(h, mix_w_in[i], mix_w_out[i], attn_sinks[i], sgu_ln_g[i], sgu_ln_b[i],
                               sgu_w[i], sgu_b[i], cos, sin)
        else:
            m = multiscale_pool_mixer(h, pool_w[i], pool_scale[i])
        x = x + rms_norm(m, g[3])
        h = rms_norm(x, g[4])
        mem_n = rms_norm(mem, mem_norm[layer])
        x = x + rms_norm(memory_cross_attention(h, mem_n, x_wq[layer], x_wk[layer], x_wv[layer], x_wo[layer]), g[5])
        h = rms_norm(x, g[6])
        x = x + 0.5 * rms_norm(swiglu(h, ffn2_wg[layer], ffn2_wu[layer], ffn2_wd[layer]), g[7])
    return x
```

```python
import functools

import jax
import jax.numpy as jnp
from jax import lax
from jax.experimental import pallas as pl
from jax.experimental.pallas import tpu as pltpu

F32 = jnp.float32
BF16 = jnp.bfloat16

EPS = 1e-6
SEQ = 4096
HEAD_DIM = 64
N_Q_HEADS = 16
N_KV_HEADS = 2
Q_PER_KV = N_Q_HEADS // N_KV_HEADS
ATTN_WIDTH = N_Q_HEADS * HEAD_DIM
BLOCK = 128
ROPE_DIM = HEAD_DIM // 4
ROPE_THETA = 500000.0
SGU_GROUPS = 8
SGU_WIDTH = 1024
POOL_WINDOWS = (2, 4, 8, 16)
POOL_HALO = 16
X_HEADS = 4
X_HEAD_DIM = 128
LANES = 128

VMEM_LIMIT_BYTES = 56 * 1024 * 1024


def _rms(x, g):
    return x * lax.rsqrt(jnp.mean(x * x, axis=-1, keepdims=True) + EPS) * g


def _params(*sem):
    return pltpu.CompilerParams(dimension_semantics=sem, vmem_limit_bytes=VMEM_LIMIT_BYTES)


def _ffn_body(x_ref, gpre_ref, gpost_ref, wg_ref, wu_ref, wd_ref, o_ref, h_ref, acc_ref):
    j = pl.program_id(1)

    @pl.when(j == 0)
    def _():
        h_ref[...] = _rms(x_ref[...], gpre_ref[...]).astype(BF16)

    h = h_ref[...]
    gate = jnp.dot(h, wg_ref[...], preferred_element_type=F32)
    up = jnp.dot(h, wu_ref[...], preferred_element_type=F32)
    a = (jax.nn.silu(gate) * up).astype(BF16)
    d = jnp.dot(a, wd_ref[...], preferred_element_type=F32)

    @pl.when(j == 0)
    def _():
        acc_ref[...] = d

    @pl.when(j > 0)
    def _():
        acc_ref[...] += d

    @pl.when(j == pl.num_programs(1) - 1)
    def _():
        o_ref[...] = x_ref[...] + 0.5 * _rms(acc_ref[...], gpost_ref[...])


def _ffn(x, g_pre, g_post, wg, wu, wd, *, tm=512, fk=512):
    t, d = x.shape
    f = wg.shape[1]
    return pl.pallas_call(
        _ffn_body,
        grid=(t // tm, f // fk),
        in_specs=[
            pl.BlockSpec((tm, d), lambda i, j: (i, 0)),
            pl.BlockSpec((1, d), lambda i, j: (0, 0)),
            pl.BlockSpec((1, d), lambda i, j: (0, 0)),
            pl.BlockSpec((d, fk), lambda i, j: (0, j)),
            pl.BlockSpec((d, fk), lambda i, j: (0, j)),
            pl.BlockSpec((fk, d), lambda i, j: (j, 0)),
        ],
        out_specs=pl.BlockSpec((tm, d), lambda i, j: (i, 0)),
        out_shape=jax.ShapeDtypeStruct((t, d), F32),
        scratch_shapes=[pltpu.VMEM((tm, d), BF16), pltpu.VMEM((tm, d), F32)],
        compiler_params=_params("parallel", "arbitrary"),
        name="ffn",
    )(x, g_pre, g_post, wg, wu, wd)


def _rope(p, cos, sin_lo, sin_hi):
    return p * cos + pltpu.roll(p, 8, 1) * sin_hi + pltpu.roll(p, LANES - 8, 1) * sin_lo


def _dup_halves(p):
    lane = lax.broadcasted_iota(jnp.int32, p.shape, 1)
    lo = jnp.where(lane < HEAD_DIM, p, 0.0)
    hi = p - lo
    return lo + pltpu.roll(lo, HEAD_DIM, 1), hi + pltpu.roll(hi, HEAD_DIM, 1)


def _mixer_in_body(x_ref, g_ref, w_ref, cos_ref, slo_ref, shi_ref, q_ref, kv_ref, u_ref, v_ref, h_ref):
    h_ref[...] = _rms(x_ref[...], g_ref[...]).astype(BF16)
    h = h_ref[...]
    cos, slo, shi = cos_ref[...], slo_ref[...], shi_ref[...]

    def proj(c0, width):
        return jnp.dot(h, w_ref[:, c0:c0 + width], preferred_element_type=F32)

    for c in range(ATTN_WIDTH // LANES):
        q_ref[:, c * LANES:(c + 1) * LANES] = _rope(proj(c * LANES, LANES), cos, slo, shi).astype(BF16)
    k = _rope(proj(ATTN_WIDTH, LANES), cos, slo, shi)
    v = proj(ATTN_WIDTH + LANES, LANES)
    for n, part in enumerate(_dup_halves(k) + _dup_halves(v)):
        kv_ref[:, n * LANES:(n + 1) * LANES] = part.astype(BF16)
    c0 = ATTN_WIDTH + 2 * LANES
    chunk = 2 * LANES
    for c in range(SGU_WIDTH // chunk):
        u_ref[:, c * chunk:(c + 1) * chunk] = jax.nn.gelu(proj(c0 + c * chunk, chunk)).astype(BF16)
        v_ref[:, c * chunk:(c + 1) * chunk] = jax.nn.gelu(proj(c0 + SGU_WIDTH + c * chunk, chunk)).astype(BF16)


def _mixer_in(x, g, w_in, cos, slo, shi, *, tm=256):
    t, d = x.shape
    n_in = w_in.shape[1]
    seq_tiles = SEQ // tm
    row = lambda i: (i, 0)
    pos = lambda i: (i % seq_tiles, 0)
    const = lambda i: (0, 0)
    return pl.pallas_call(
        _mixer_in_body,
        grid=(t // tm,),
        in_specs=[
            pl.BlockSpec((tm, d), row),
            pl.BlockSpec((1, d), const),
            pl.BlockSpec((d, n_in), const),
            pl.BlockSpec((tm, LANES), pos),
            pl.BlockSpec((tm, LANES), pos),
            pl.BlockSpec((tm, LANES), pos),
        ],
        out_specs=[
            pl.BlockSpec((tm, ATTN_WIDTH), row),
            pl.BlockSpec((tm, 4 * LANES), row),
            pl.BlockSpec((tm, SGU_WIDTH), row),
            pl.BlockSpec((tm, SGU_WIDTH), row),
        ],
        out_shape=[
            jax.ShapeDtypeStruct((t, ATTN_WIDTH), BF16),
            jax.ShapeDtypeStruct((t, 4 * LANES), BF16),
            jax.ShapeDtypeStruct((t, SGU_WIDTH), BF16),
            jax.ShapeDtypeStruct((t, SGU_WIDTH), BF16),
        ],
        scratch_shapes=[pltpu.VMEM((tm, d), BF16)],
        compiler_params=_params("parallel"),
        name="mixer_in",
    )(x, g, w_in, cos, slo, shi)


def _swa_block(q, kv_prev, kv_cur, sinks_ref, not_first):
    lane = lax.broadcasted_iota(jnp.int32, (BLOCK, LANES), 1)
    lo_half = lane < HEAD_DIM
    rows = Q_PER_KV * BLOCK
    qi = lax.broadcasted_iota(jnp.int32, (rows, 2 * BLOCK), 0) & (BLOCK - 1)
    kj = lax.broadcasted_iota(jnp.int32, (rows, 2 * BLOCK), 1)
    rel = qi + BLOCK - kj
    first_key = jnp.where(not_first, 0, BLOCK)
    valid = (rel >= 0) & (rel < BLOCK) & (kj >= first_key)
    zero = jnp.zeros((), BF16)
    outs = []
    for g in range(N_KV_HEADS):
        kd = jnp.concatenate([kv_prev[:, g * LANES:(g + 1) * LANES], kv_cur[:, g * LANES:(g + 1) * LANES]], axis=0)
        vd = jnp.concatenate([kv_prev[:, (2 + g) * LANES:(3 + g) * LANES],
                              kv_cur[:, (2 + g) * LANES:(3 + g) * LANES]], axis=0)
        parts, sink_parts = [], []
        for p in range(Q_PER_KV // 2):
            pair = q[:, (g * Q_PER_KV // 2 + p) * LANES:(g * Q_PER_KV // 2 + p + 1) * LANES]
            parts.append(jnp.where(lo_half, pair, zero))
            parts.append(jnp.where(lo_half, zero, pair))
        for hq in range(Q_PER_KV):
            sink_parts.append(jnp.full((BLOCK, 1), sinks_ref[g * Q_PER_KV + hq], F32))
        qs = jnp.concatenate(parts, axis=0)
        sink = jnp.concatenate(sink_parts, axis=0)
        s = lax.dot_general(qs, kd, (((1,), (1,)), ((), ())), preferred_element_type=F32)
        s = jnp.where(valid, s * (HEAD_DIM ** -0.5), -1e30)
        m = jnp.maximum(jnp.max(s, axis=-1, keepdims=True), sink)
        e = jnp.exp(s - m)
        denom = jnp.sum(e, axis=-1, keepdims=True) + jnp.exp(sink - m)
        o = jnp.dot(e.astype(BF16), vd, preferred_element_type=F32) * (1.0 / denom)
        for p in range(Q_PER_KV // 2):
            even = o[(2 * p) * BLOCK:(2 * p + 1) * BLOCK]
            odd = o[(2 * p + 1) * BLOCK:(2 * p + 2) * BLOCK]
            outs.append(jnp.where(lo_half, even, odd).astype(BF16))
    return jnp.concatenate(outs, axis=1)


def _sgu_block(u, v, ln_g, ln_b, w_ref, bt_ref):
    vf = v.astype(F32)
    mu = jnp.mean(vf, axis=-1, keepdims=True)
    vc = vf - mu
    var = jnp.mean(vc * vc, axis=-1, keepdims=True)
    vn = (vc * lax.rsqrt(var + EPS) * ln_g + ln_b).astype(BF16)
    ri = lax.broadcasted_iota(jnp.int32, (BLOCK, BLOCK), 0)
    ci = lax.broadcasted_iota(jnp.int32, (BLOCK, BLOCK), 1)
    causal = ci <= ri
    outs = []
    for g in range(SGU_GROUPS):
        w = jnp.where(causal, w_ref[g], 0.0).astype(BF16)
        mixed = jnp.dot(w, vn[:, g * LANES:(g + 1) * LANES], preferred_element_type=F32) + bt_ref[:, g:g + 1]
        outs.append((u[:, g * LANES:(g + 1) * LANES].astype(F32) * mixed).astype(BF16))
    return jnp.concatenate(outs, axis=1)


def _mixer_out_body(sinks_ref, x_ref, q_ref, kv_ref, kvp_ref, u_ref, v_ref, lng_ref, lnb_ref, sw_ref, sbt_ref,
                    wout_ref, g_ref, o_ref, mix_ref):
    tm = x_ref.shape[0]
    nblk = tm // BLOCK
    blocks_per_seq = SEQ // BLOCK
    first_blk = pl.program_id(0) * nblk
    for b in range(nblk):
        r0 = b * BLOCK
        kv_cur = kv_ref[r0:r0 + BLOCK, :]
        kv_prev = kvp_ref[...] if b == 0 else kv_ref[r0 - BLOCK:r0, :]
        not_first = ((first_blk + b) % blocks_per_seq) != 0
        mix_ref[r0:r0 + BLOCK, :ATTN_WIDTH] = _swa_block(q_ref[r0:r0 + BLOCK, :], kv_prev, kv_cur, sinks_ref, not_first)
        mix_ref[r0:r0 + BLOCK, ATTN_WIDTH:] = _sgu_block(u_ref[r0:r0 + BLOCK, :], v_ref[r0:r0 + BLOCK, :],
                                                        lng_ref[...], lnb_ref[...], sw_ref, sbt_ref)
    m = jnp.dot(mix_ref[...], wout_ref[...], preferred_element_type=F32)
    o_ref[...] = x_ref[...] + _rms(m, g_ref[...])


def _mixer_out(x, q, kv, u, v, sinks, ln_g, ln_b, sgu_w, sgu_bt, w_out, g, *, tm=256):
    t, d = x.shape
    nblk = tm // BLOCK
    row = lambda i: (i, 0)
    const = lambda i: (0, 0)
    return pl.pallas_call(
        _mixer_out_body,
        grid=(t // tm,),
        in_specs=[
            pl.BlockSpec(memory_space=pltpu.SMEM),
            pl.BlockSpec((tm, d), row),
            pl.BlockSpec((tm, ATTN_WIDTH), row),
            pl.BlockSpec((tm, 4 * LANES), row),
            pl.BlockSpec((BLOCK, 4 * LANES), lambda i: (jnp.maximum(i * nblk - 1, 0), 0)),
            pl.BlockSpec((tm, SGU_WIDTH), row),
            pl.BlockSpec((tm, SGU_WIDTH), row),
            pl.BlockSpec((1, SGU_WIDTH), const),
            pl.BlockSpec((1, SGU_WIDTH), const),
            pl.BlockSpec((SGU_GROUPS, BLOCK, BLOCK), lambda i: (0, 0, 0)),
            pl.BlockSpec((BLOCK, SGU_GROUPS), const),
            pl.BlockSpec((ATTN_WIDTH + SGU_WIDTH, d), const),
            pl.BlockSpec((1, d), const),
        ],
        out_specs=pl.BlockSpec((tm, d), row),
        out_shape=jax.ShapeDtypeStruct((t, d), F32),
        scratch_shapes=[pltpu.VMEM((tm, ATTN_WIDTH + SGU_WIDTH), BF16)],
        compiler_params=_params("parallel"),
        name="mixer_out",
    )(sinks, x, q, kv, kv, u, v, ln_g, ln_b, sgu_w, sgu_bt, w_out, g)


def _pool_body(x_ref, halo_ref, gpre_ref, w_ref, scale_ref, gpost_ref, o_ref):
    tm, d = x_ref.shape
    gd = d // len(POOL_WINDOWS)
    seq_start = (pl.program_id(0) * tm) % SEQ == 0
    x = x_ref[...]
    h = _rms(x, gpre_ref[...])
    h_halo = jnp.where(seq_start, 0.0, _rms(halo_ref[...], gpre_ref[...]))
    ext = jnp.concatenate([h_halo, h], axis=0)
    t_pos = (pl.program_id(0) * tm) % SEQ + lax.broadcasted_iota(jnp.int32, (tm, 1), 0)
    count = (t_pos + 1).astype(F32)
    ys = []
    for gi, w in enumerate(POOL_WINDOWS):
        s = ext[:, gi * gd:(gi + 1) * gd]
        span = 1
        while span < w:
            s = s + pltpu.roll(s, span, 0)
            span *= 2
        mean = s[POOL_HALO:] * (1.0 / jnp.minimum(count, float(w)))
        pooled = (mean - h[:, gi * gd:(gi + 1) * gd]).astype(BF16)
        ys.append(jnp.dot(pooled, w_ref[gi], preferred_element_type=F32))
    y = jnp.concatenate(ys, axis=1) * scale_ref[...]
    o_ref[...] = x + _rms(y, gpost_ref[...])


def _pool(x, g_pre, pool_w, pool_scale, g_post, *, tm=512):
    t, d = x.shape
    gd = d // len(POOL_WINDOWS)
    halo_blocks = tm // POOL_HALO
    row = lambda i: (i, 0)
    const = lambda i: (0, 0)
    return pl.pallas_call(
        _pool_body,
        grid=(t // tm,),
        in_specs=[
            pl.BlockSpec((tm, d), row),
            pl.BlockSpec((POOL_HALO, d), lambda i: (jnp.maximum(i * halo_blocks - 1, 0), 0)),
            pl.BlockSpec((1, d), const),
            pl.BlockSpec((len(POOL_WINDOWS), gd, gd), lambda i: (0, 0, 0)),
            pl.BlockSpec((1, d), const),
            pl.BlockSpec((1, d), const),
        ],
        out_specs=pl.BlockSpec((tm, d), row),
        out_shape=jax.ShapeDtypeStruct((t, d), F32),
        compiler_params=_params("parallel"),
        name="pool",
    )(x, x, g_pre, pool_w, pool_scale, g_post)


def _mem_kv_body(m_ref, g_ref, wk_ref, wv_ref, k_ref, v_ref):
    h = _rms(m_ref[...], g_ref[...]).astype(BF16)
    k_ref[...] = jnp.dot(h, wk_ref[...], preferred_element_type=F32).astype(BF16)
    v_ref[...] = jnp.dot(h, wv_ref[...], preferred_element_type=F32).astype(BF16)


def _mem_kv(mem, g, wk, wv, *, tm=256):
    t, d = mem.shape
    n = wk.shape[1]
    row = lambda i: (i, 0)
    const = lambda i: (0, 0)
    return pl.pallas_call(
        _mem_kv_body,
        grid=(t // tm,),
        in_specs=[pl.BlockSpec((tm, d), row), pl.BlockSpec((1, d), const),
                  pl.BlockSpec((d, n), const), pl.BlockSpec((d, n), const)],
        out_specs=[pl.BlockSpec((tm, n), row), pl.BlockSpec((tm, n), row)],
        out_shape=[jax.ShapeDtypeStruct((t, n), BF16), jax.ShapeDtypeStruct((t, n), BF16)],
        compiler_params=_params("parallel"),
        name="mem_kv",
    )(mem, g, wk, wv)


def _cross_body(x_ref, gpre_ref, wq_ref, k_ref, v_ref, wo_ref, gpost_ref, o_ref):
    x = x_ref[...]
    h = _rms(x, gpre_ref[...]).astype(BF16)
    q = jnp.dot(h, wq_ref[...], preferred_element_type=F32).astype(BF16)
    outs = []
    for hh in range(X_HEADS):
        sl = slice(hh * X_HEAD_DIM, (hh + 1) * X_HEAD_DIM)
        s = lax.dot_general(q[:, sl], k_ref[:, sl], (((1,), (1,)), ((), ())),
                            preferred_element_type=F32) * (X_HEAD_DIM ** -0.5)
        e = jnp.exp(s - jnp.max(s, axis=-1, keepdims=True))
        denom = jnp.sum(e, axis=-1, keepdims=True)
        o = jnp.dot(e.astype(BF16), v_ref[:, sl], preferred_element_type=F32) * (1.0 / denom)
        outs.append(o.astype(BF16))
    y = jnp.dot(jnp.concatenate(outs, axis=1), wo_ref[...], preferred_element_type=F32)
    o_ref[...] = x + _rms(y, gpost_ref[...])


def _cross(x, g_pre, wq, k, v, wo, g_post, *, tm=512):
    t, d = x.shape
    n = wq.shape[1]
    mem_len = k.shape[0] // (t // SEQ)
    tiles_per_seq = SEQ // tm
    row = lambda i: (i, 0)
    const = lambda i: (0, 0)
    batch = lambda i: (i // tiles_per_seq, 0)
    return pl.pallas_call(
        _cross_body,
        grid=(t // tm,),
        in_specs=[
            pl.BlockSpec((tm, d), row),
            pl.BlockSpec((1, d), const),
            pl.BlockSpec((d, n), const),
            pl.BlockSpec((mem_len, n), batch),
            pl.BlockSpec((mem_len, n), batch),
            pl.BlockSpec((n, d), const),
            pl.BlockSpec((1, d), const),
        ],
        out_specs=pl.BlockSpec((tm, d), row),
        out_shape=jax.ShapeDtypeStruct((t, d), F32),
        compiler_params=_params("parallel"),
        name="cross",
    )(x, g_pre, wq, k, v, wo, g_post)


def _rope_tables():
    half = ROPE_DIM // 2
    inv = ROPE_THETA ** (-jnp.arange(half, dtype=F32) * 2.0 / ROPE_DIM)
    ang = jnp.arange(SEQ, dtype=F32)[:, None] * inv[None, :]
    cos, sin = jnp.cos(ang), jnp.sin(ang)
    ones = jnp.ones((SEQ, HEAD_DIM - ROPE_DIM), F32)
    zeros = jnp.zeros((SEQ, HEAD_DIM - ROPE_DIM), F32)
    z8 = jnp.zeros((SEQ, half), F32)
    cos_h = jnp.concatenate([cos, cos, ones], axis=1)
    slo_h = jnp.concatenate([-sin, z8, zeros], axis=1)
    shi_h = jnp.concatenate([z8, sin, zeros], axis=1)
    two = lambda a: jnp.concatenate([a, a], axis=1)
    return two(cos_h), two(slo_h), two(shi_h)


def kernel(x, mem, norms, mem_norm, ffn1_wg, ffn1_wu, ffn1_wd, ffn2_wg, ffn2_wu, ffn2_wd, x_wq, x_wk, x_wv, x_wo,
           mix_w_in, mix_w_out, attn_sinks, sgu_ln_g, sgu_ln_b, sgu_w, sgu_b, pool_w, pool_scale):
    b, s, d = x.shape
    depth = norms.shape[0]
    assert s == SEQ
    bf = lambda a: a.astype(BF16)
    xt = x.reshape(b * s, d)
    memt = mem.reshape(b * mem.shape[1], d)
    cos, slo, shi = _rope_tables()
    for layer in range(depth):
        g = norms[layer][:, None, :]
        xt = _ffn(xt, g[0], g[1], bf(ffn1_wg[layer]), bf(ffn1_wu[layer]), bf(ffn1_wd[layer]))
        i = layer // 2
        if layer % 2 == 0:
            q, kv, u, v = _mixer_in(xt, g[2], bf(mix_w_in[i]), cos, slo, shi)
            xt = _mixer_out(xt, q, kv, u, v, attn_sinks[i], sgu_ln_g[i][None], sgu_ln_b[i][None], sgu_w[i],
                            sgu_b[i].T, bf(mix_w_out[i]), g[3])
        else:
            xt = _pool(xt, g[2], bf(pool_w[i]), pool_scale[i][None], g[3])
        kx, vx = _mem_kv(memt, mem_norm[layer][None], bf(x_wk[layer]), bf(x_wv[layer]))
        xt = _cross(xt, g[4], bf(x_wq[layer]), kx, vx, bf(x_wo[layer]), g[5])
        xt = _ffn(xt, g[6], g[7], bf(ffn2_wg[layer]), bf(ffn2_wu[layer]), bf(ffn2_wd[layer]))
    return xt.reshape(b, s, d)
```

```python
import functools

import jax
import jax.numpy as jnp
from jax import lax
from jax.experimental import pallas as pl
from jax.experimental.pallas import tpu as pltpu

F32 = jnp.float32
BF16 = jnp.bfloat16

EPS = 1e-6
SEQ = 4096
HEAD_DIM = 64
N_Q_HEADS = 16
N_KV_HEADS = 2
Q_PER_KV = N_Q_HEADS // N_KV_HEADS
ATTN_WIDTH = N_Q_HEADS * HEAD_DIM
BLOCK = 128
ROPE_DIM = HEAD_DIM // 4
ROPE_THETA = 500000.0
SGU_GROUPS = 8
SGU_WIDTH = 1024
POOL_WINDOWS = (2, 4, 8, 16)
POOL_HALO = 16
X_HEADS = 4
X_HEAD_DIM = 128
LANES = 128
FFN_SUB = 256

VMEM_LIMIT_BYTES = 56 * 1024 * 1024


def _rms(x, g):
    return x * lax.rsqrt(jnp.mean(x * x, axis=-1, keepdims=True) + EPS) * g


def _params(*sem):
    return pltpu.CompilerParams(dimension_semantics=sem, vmem_limit_bytes=VMEM_LIMIT_BYTES)


def _ffn_body(x_ref, gpre_ref, gpost_ref, wg_ref, wu_ref, wd_ref, o_ref, h_ref, acc_ref):
    j = pl.program_id(1)

    @pl.when(j == 0)
    def _():
        h_ref[...] = _rms(x_ref[...], gpre_ref[...]).astype(BF16)
        acc_ref[...] = jnp.zeros_like(acc_ref)

    h = h_ref[...]
    fk = wg_ref.shape[1]
    for c0 in range(0, fk, FFN_SUB):
        gate = jnp.dot(h, wg_ref[:, c0:c0 + FFN_SUB], preferred_element_type=F32)
        up = jnp.dot(h, wu_ref[:, c0:c0 + FFN_SUB], preferred_element_type=F32)
        a = (jax.nn.silu(gate) * up).astype(BF16)
        acc_ref[...] += jnp.dot(a, wd_ref[c0:c0 + FFN_SUB, :], preferred_element_type=F32)

    @pl.when(j == pl.num_programs(1) - 1)
    def _():
        o_ref[...] = x_ref[...] + 0.5 * _rms(acc_ref[...], gpost_ref[...])


def _ffn(x, g_pre, g_post, wg, wu, wd, layer, *, tm=512, fk=512):
    t, d = x.shape
    f = wg.shape[2]
    return pl.pallas_call(
        _ffn_body,
        grid=(t // tm, f // fk),
        in_specs=[
            pl.BlockSpec((tm, d), lambda i, j: (i, 0)),
            pl.BlockSpec((1, d), lambda i, j: (0, 0)),
            pl.BlockSpec((1, d), lambda i, j: (0, 0)),
            pl.BlockSpec((None, d, fk), lambda i, j: (layer, 0, j)),
            pl.BlockSpec((None, d, fk), lambda i, j: (layer, 0, j)),
            pl.BlockSpec((None, fk, d), lambda i, j: (layer, j, 0)),
        ],
        out_specs=pl.BlockSpec((tm, d), lambda i, j: (i, 0)),
        out_shape=jax.ShapeDtypeStruct((t, d), F32),
        scratch_shapes=[pltpu.VMEM((tm, d), BF16), pltpu.VMEM((tm, d), F32)],
        compiler_params=_params("parallel", "arbitrary"),
        name="ffn",
    )(x, g_pre, g_post, wg, wu, wd)


def _rope(p, cos, sin_lo, sin_hi):
    return p * cos + pltpu.roll(p, 8, 1) * sin_hi + pltpu.roll(p, LANES - 8, 1) * sin_lo


def _dup_halves(p):
    lane = lax.broadcasted_iota(jnp.int32, p.shape, 1)
    lo = jnp.where(lane < HEAD_DIM, p, 0.0)
    hi = p - lo
    return lo + pltpu.roll(lo, HEAD_DIM, 1), hi + pltpu.roll(hi, HEAD_DIM, 1)


def _mixer_in_body(x_ref, g_ref, w_ref, cos_ref, slo_ref, shi_ref, q_ref, kv_ref, u_ref, v_ref, h_ref):
    h_ref[...] = _rms(x_ref[...], g_ref[...]).astype(BF16)
    h = h_ref[...]
    cos, slo, shi = cos_ref[...], slo_ref[...], shi_ref[...]

    def proj(c0, width):
        return jnp.dot(h, w_ref[:, c0:c0 + width], preferred_element_type=F32)

    chunk = 2 * LANES
    for c in range(ATTN_WIDTH // chunk):
        z = proj(c * chunk, chunk)
        for n in range(2):
            q_ref[:, c * chunk + n * LANES:c * chunk + (n + 1) * LANES] = _rope(
                z[:, n * LANES:(n + 1) * LANES], cos, slo, shi).astype(BF16)
    z = proj(ATTN_WIDTH, chunk)
    k = _rope(z[:, :LANES], cos, slo, shi)
    v = z[:, LANES:]
    for n, part in enumerate(_dup_halves(k) + _dup_halves(v)):
        kv_ref[:, n * LANES:(n + 1) * LANES] = part.astype(BF16)
    c0 = ATTN_WIDTH + 2 * LANES
    for c in range(SGU_WIDTH // chunk):
        u_ref[:, c * chunk:(c + 1) * chunk] = jax.nn.gelu(proj(c0 + c * chunk, chunk)).astype(BF16)
        v_ref[:, c * chunk:(c + 1) * chunk] = jax.nn.gelu(proj(c0 + SGU_WIDTH + c * chunk, chunk)).astype(BF16)


def _mixer_in(x, g, w_in, cos, slo, shi, *, tm=256):
    t, d = x.shape
    n_in = w_in.shape[1]
    seq_tiles = SEQ // tm
    row = lambda i: (i, 0)
    pos = lambda i: (i % seq_tiles, 0)
    const = lambda i: (0, 0)
    return pl.pallas_call(
        _mixer_in_body,
        grid=(t // tm,),
        in_specs=[
            pl.BlockSpec((tm, d), row),
            pl.BlockSpec((1, d), const),
            pl.BlockSpec((d, n_in), const),
            pl.BlockSpec((tm, LANES), pos),
            pl.BlockSpec((tm, LANES), pos),
            pl.BlockSpec((tm, LANES), pos),
        ],
        out_specs=[
            pl.BlockSpec((tm, ATTN_WIDTH), row),
            pl.BlockSpec((tm, 4 * LANES), row),
            pl.BlockSpec((tm, SGU_WIDTH), row),
            pl.BlockSpec((tm, SGU_WIDTH), row),
        ],
        out_shape=[
            jax.ShapeDtypeStruct((t, ATTN_WIDTH), BF16),
            jax.ShapeDtypeStruct((t, 4 * LANES), BF16),
            jax.ShapeDtypeStruct((t, SGU_WIDTH), BF16),
            jax.ShapeDtypeStruct((t, SGU_WIDTH), BF16),
        ],
        scratch_shapes=[pltpu.VMEM((tm, d), BF16)],
        compiler_params=_params("parallel"),
        name="mixer_in",
    )(x, g, w_in, cos, slo, shi)


def _swa_block(q, kv_prev, kv_cur, sinks_ref, not_first):
    lane = lax.broadcasted_iota(jnp.int32, (BLOCK, LANES), 1)
    lo_half = lane < HEAD_DIM
    rows = Q_PER_KV * BLOCK
    qi = lax.broadcasted_iota(jnp.int32, (rows, 2 * BLOCK), 0) & (BLOCK - 1)
    kj = lax.broadcasted_iota(jnp.int32, (rows, 2 * BLOCK), 1)
    rel = qi + BLOCK - kj
    first_key = jnp.where(not_first, 0, BLOCK)
    valid = (rel >= 0) & (rel < BLOCK) & (kj >= first_key)
    zero = jnp.zeros((), BF16)
    outs = []
    for g in range(N_KV_HEADS):
        kd = jnp.concatenate([kv_prev[:, g * LANES:(g + 1) * LANES], kv_cur[:, g * LANES:(g + 1) * LANES]], axis=0)
        vd = jnp.concatenate([kv_prev[:, (2 + g) * LANES:(3 + g) * LANES],
                              kv_cur[:, (2 + g) * LANES:(3 + g) * LANES]], axis=0)
        parts, sink_parts = [], []
        for p in range(Q_PER_KV // 2):
            pair = q[:, (g * Q_PER_KV // 2 + p) * LANES:(g * Q_PER_KV // 2 + p + 1) * LANES]
            parts.append(jnp.where(lo_half, pair, zero))
            parts.append(jnp.where(lo_half, zero, pair))
        for hq in range(Q_PER_KV):
            sink_parts.append(jnp.full((BLOCK, 1), sinks_ref[g * Q_PER_KV + hq], F32))
        qs = jnp.concatenate(parts, axis=0)
        sink = jnp.concatenate(sink_parts, axis=0)
        s = lax.dot_general(qs, kd, (((1,), (1,)), ((), ())), preferred_element_type=F32)
        s = jnp.where(valid, s * (HEAD_DIM ** -0.5), -1e30)
        m = jnp.maximum(jnp.max(s, axis=-1, keepdims=True), sink)
        e = jnp.exp(s - m)
        denom = jnp.sum(e, axis=-1, keepdims=True) + jnp.exp(sink - m)
        o = jnp.dot(e.astype(BF16), vd, preferred_element_type=F32) * (1.0 / denom)
        for p in range(Q_PER_KV // 2):
            even = o[(2 * p) * BLOCK:(2 * p + 1) * BLOCK]
            odd = o[(2 * p + 1) * BLOCK:(2 * p + 2) * BLOCK]
            outs.append(jnp.where(lo_half, even, odd).astype(BF16))
    return jnp.concatenate(outs, axis=1)


def _sgu_block(u, v, ln_g, ln_b, w_ref, bt_ref):
    vf = v.astype(F32)
    mu = jnp.mean(vf, axis=-1, keepdims=True)
    vc = vf - mu
    var = jnp.mean(vc * vc, axis=-1, keepdims=True)
    vn = (vc * lax.rsqrt(var + EPS) * ln_g + ln_b).astype(BF16)
    ri = lax.broadcasted_iota(jnp.int32, (BLOCK, BLOCK), 0)
    ci = lax.broadcasted_iota(jnp.int32, (BLOCK, BLOCK), 1)
    causal = ci <= ri
    outs = []
    for g in range(SGU_GROUPS):
        w = jnp.where(causal, w_ref[g], 0.0).astype(BF16)
        mixed = jnp.dot(w, vn[:, g * LANES:(g + 1) * LANES], preferred_element_type=F32) + bt_ref[:, g:g + 1]
        outs.append((u[:, g * LANES:(g + 1) * LANES].astype(F32) * mixed).astype(BF16))
    return jnp.concatenate(outs, axis=1)


def _mixer_out_body(sinks_ref, x_ref, q_ref, kv_ref, kvp_ref, u_ref, v_ref, lng_ref, lnb_ref, sw_ref, sbt_ref,
                    wout_ref, g_ref, o_ref, mix_ref):
    tm = x_ref.shape[0]
    nblk = tm // BLOCK
    blocks_per_seq = SEQ // BLOCK
    first_blk = pl.program_id(0) * nblk
    for b in range(nblk):
        r0 = b * BLOCK
        kv_cur = kv_ref[r0:r0 + BLOCK, :]
        kv_prev = kvp_ref[...] if b == 0 else kv_ref[r0 - BLOCK:r0, :]
        not_first = ((first_blk + b) % blocks_per_seq) != 0
        mix_ref[r0:r0 + BLOCK, :ATTN_WIDTH] = _swa_block(q_ref[r0:r0 + BLOCK, :], kv_prev, kv_cur, sinks_ref, not_first)
        mix_ref[r0:r0 + BLOCK, ATTN_WIDTH:] = _sgu_block(u_ref[r0:r0 + BLOCK, :], v_ref[r0:r0 + BLOCK, :],
                                                        lng_ref[...], lnb_ref[...], sw_ref, sbt_ref)
    m = jnp.dot(mix_ref[...], wout_ref[...], preferred_element_type=F32)
    o_ref[...] = x_ref[...] + _rms(m, g_ref[...])


def _mixer_out(x, q, kv, u, v, sinks, ln_g, ln_b, sgu_w, sgu_bt, w_out, g, *, tm=256):
    t, d = x.shape
    nblk = tm // BLOCK
    row = lambda i: (i, 0)
    const = lambda i: (0, 0)
    return pl.pallas_call(
        _mixer_out_body,
        grid=(t // tm,),
        in_specs=[
            pl.BlockSpec(memory_space=pltpu.SMEM),
            pl.BlockSpec((tm, d), row),
            pl.BlockSpec((tm, ATTN_WIDTH), row),
            pl.BlockSpec((tm, 4 * LANES), row),
            pl.BlockSpec((BLOCK, 4 * LANES), lambda i: (jnp.maximum(i * nblk - 1, 0), 0)),
            pl.BlockSpec((tm, SGU_WIDTH), row),
            pl.BlockSpec((tm, SGU_WIDTH), row),
            pl.BlockSpec((1, SGU_WIDTH), const),
            pl.BlockSpec((1, SGU_WIDTH), const),
            pl.BlockSpec((SGU_GROUPS, BLOCK, BLOCK), lambda i: (0, 0, 0)),
            pl.BlockSpec((BLOCK, SGU_GROUPS), const),
            pl.BlockSpec((ATTN_WIDTH + SGU_WIDTH, d), const),
            pl.BlockSpec((1, d), const),
        ],
        out_specs=pl.BlockSpec((tm, d), row),
        out_shape=jax.ShapeDtypeStruct((t, d), F32),
        scratch_shapes=[pltpu.VMEM((tm, ATTN_WIDTH + SGU_WIDTH), BF16)],
        compiler_params=_params("parallel"),
        name="mixer_out",
    )(sinks, x, q, kv, kv, u, v, ln_g, ln_b, sgu_w, sgu_bt, w_out, g)


def _pool_body(x_ref, halo_ref, gpre_ref, w_ref, scale_ref, gpost_ref, o_ref):
    tm, d = x_ref.shape
    gd = d // len(POOL_WINDOWS)
    seq_start = (pl.program_id(0) * tm) % SEQ == 0
    x = x_ref[...]
    h = _rms(x, gpre_ref[...])
    h_halo = jnp.where(seq_start, 0.0, _rms(halo_ref[...], gpre_ref[...]))
    ext = jnp.concatenate([h_halo, h], axis=0)
    t_pos = (pl.program_id(0) * tm) % SEQ + lax.broadcasted_iota(jnp.int32, (tm, 1), 0)
    count = (t_pos + 1).astype(F32)
    ys = []
    for gi, w in enumerate(POOL_WINDOWS):
        s = ext[:, gi * gd:(gi + 1) * gd]
        span = 1
        while span < w:
            s = s + pltpu.roll(s, span, 0)
            span *= 2
        mean = s[POOL_HALO:] * (1.0 / jnp.minimum(count, float(w)))
        pooled = (mean - h[:, gi * gd:(gi + 1) * gd]).astype(BF16)
        ys.append(jnp.dot(pooled, w_ref[gi], preferred_element_type=F32))
    y = jnp.concatenate(ys, axis=1) * scale_ref[...]
    o_ref[...] = x + _rms(y, gpost_ref[...])


def _pool(x, g_pre, pool_w, pool_scale, g_post, *, tm=512):
    t, d = x.shape
    gd = d // len(POOL_WINDOWS)
    halo_blocks = tm // POOL_HALO
    row = lambda i: (i, 0)
    const = lambda i: (0, 0)
    return pl.pallas_call(
        _pool_body,
        grid=(t // tm,),
        in_specs=[
            pl.BlockSpec((tm, d), row),
            pl.BlockSpec((POOL_HALO, d), lambda i: (jnp.maximum(i * halo_blocks - 1, 0), 0)),
            pl.BlockSpec((1, d), const),
            pl.BlockSpec((len(POOL_WINDOWS), gd, gd), lambda i: (0, 0, 0)),
            pl.BlockSpec((1, d), const),
            pl.BlockSpec((1, d), const),
        ],
        out_specs=pl.BlockSpec((tm, d), row),
        out_shape=jax.ShapeDtypeStruct((t, d), F32),
        compiler_params=_params("parallel"),
        name="pool",
    )(x, x, g_pre, pool_w, pool_scale, g_post)


def _mem_kv_body(m_ref, g_ref, wk_ref, wv_ref, k_ref, v_ref):
    h = _rms(m_ref[...], g_ref[...]).astype(BF16)
    k_ref[...] = jnp.dot(h, wk_ref[...], preferred_element_type=F32).astype(BF16)
    v_ref[...] = jnp.dot(h, wv_ref[...], preferred_element_type=F32).astype(BF16)


def _mem_kv(mem, g, wk, wv, *, tm=256):
    t, d = mem.shape
    n = wk.shape[1]
    row = lambda i: (i, 0)
    const = lambda i: (0, 0)
    return pl.pallas_call(
        _mem_kv_body,
        grid=(t // tm,),
        in_specs=[pl.BlockSpec((tm, d), row), pl.BlockSpec((1, d), const),
                  pl.BlockSpec((d, n), const), pl.BlockSpec((d, n), const)],
        out_specs=[pl.BlockSpec((tm, n), row), pl.BlockSpec((tm, n), row)],
        out_shape=[jax.ShapeDtypeStruct((t, n), BF16), jax.ShapeDtypeStruct((t, n), BF16)],
        compiler_params=_params("parallel"),
        name="mem_kv",
    )(mem, g, wk, wv)


def _cross_body(x_ref, gpre_ref, wq_ref, k_ref, v_ref, wo_ref, gpost_ref, o_ref):
    x = x_ref[...]
    h = _rms(x, gpre_ref[...]).astype(BF16)
    q = jnp.dot(h, wq_ref[...], preferred_element_type=F32).astype(BF16)
    outs = []
    for hh in range(X_HEADS):
        sl = slice(hh * X_HEAD_DIM, (hh + 1) * X_HEAD_DIM)
        s = lax.dot_general(q[:, sl], k_ref[:, sl], (((1,), (1,)), ((), ())),
                            preferred_element_type=F32) * (X_HEAD_DIM ** -0.5)
        e = jnp.exp(s - jnp.max(s, axis=-1, keepdims=True))
        denom = jnp.sum(e, axis=-1, keepdims=True)
        o = jnp.dot(e.astype(BF16), v_ref[:, sl], preferred_element_type=F32) * (1.0 / denom)
        outs.append(o.astype(BF16))
    y = jnp.dot(jnp.concatenate(outs, axis=1), wo_ref[...], preferred_element_type=F32)
    o_ref[...] = x + _rms(y, gpost_ref[...])


def _cross(x, g_pre, wq, k, v, wo, g_post, *, tm=512):
    t, d = x.shape
    n = wq.shape[1]
    mem_len = k.shape[0] // (t // SEQ)
    tiles_per_seq = SEQ // tm
    row = lambda i: (i, 0)
    const = lambda i: (0, 0)
    batch = lambda i: (i // tiles_per_seq, 0)
    return pl.pallas_call(
        _cross_body,
        grid=(t // tm,),
        in_specs=[
            pl.BlockSpec((tm, d), row),
            pl.BlockSpec((1, d), const),
            pl.BlockSpec((d, n), const),
            pl.BlockSpec((mem_len, n), batch),
            pl.BlockSpec((mem_len, n), batch),
            pl.BlockSpec((n, d), const),
            pl.BlockSpec((1, d), const),
        ],
        out_specs=pl.BlockSpec((tm, d), row),
        out_shape=jax.ShapeDtypeStruct((t, d), F32),
        compiler_params=_params("parallel"),
        name="cross",
    )(x, g_pre, wq, k, v, wo, g_post)


def _rope_tables():
    half = ROPE_DIM // 2
    inv = ROPE_THETA ** (-jnp.arange(half, dtype=F32) * 2.0 / ROPE_DIM)
    ang = jnp.arange(SEQ, dtype=F32)[:, None] * inv[None, :]
    cos, sin = jnp.cos(ang), jnp.sin(ang)
    ones = jnp.ones((SEQ, HEAD_DIM - ROPE_DIM), F32)
    zeros = jnp.zeros((SEQ, HEAD_DIM - ROPE_DIM), F32)
    z8 = jnp.zeros((SEQ, half), F32)
    cos_h = jnp.concatenate([cos, cos, ones], axis=1)
    slo_h = jnp.concatenate([-sin, z8, zeros], axis=1)
    shi_h = jnp.concatenate([z8, sin, zeros], axis=1)
    two = lambda a: jnp.concatenate([a, a], axis=1)
    return two(cos_h), two(slo_h), two(shi_h)


def kernel(x, mem, norms, mem_norm, ffn1_wg, ffn1_wu, ffn1_wd, ffn2_wg, ffn2_wu, ffn2_wd, x_wq, x_wk, x_wv, x_wo,
           mix_w_in, mix_w_out, attn_sinks, sgu_ln_g, sgu_ln_b, sgu_w, sgu_b, pool_w, pool_scale):
    b, s, d = x.shape
    depth = norms.shape[0]
    assert s == SEQ
    bf = lambda a: a.astype(BF16)
    xt = x.reshape(b * s, d)
    memt = mem.reshape(b * mem.shape[1], d)
    cos, slo, shi = _rope_tables()
    ffn1 = (bf(ffn1_wg), bf(ffn1_wu), bf(ffn1_wd))
    ffn2 = (bf(ffn2_wg), bf(ffn2_wu), bf(ffn2_wd))
    for layer in range(depth):
        g = norms[layer][:, None, :]
        xt = _ffn(xt, g[0], g[1], *ffn1, layer)
        i = layer // 2
        if layer % 2 == 0:
            q, kv, u, v = _mixer_in(xt, g[2], bf(mix_w_in[i]), cos, slo, shi)
            xt = _mixer_out(xt, q, kv, u, v, attn_sinks[i], sgu_ln_g[i][None], sgu_ln_b[i][None], sgu_w[i],
                            sgu_b[i].T, bf(mix_w_out[i]), g[3])
        else:
            xt = _pool(xt, g[2], bf(pool_w[i]), pool_scale[i][None], g[3])
        kx, vx = _mem_kv(memt, mem_norm[layer][None], bf(x_wk[layer]), bf(x_wv[layer]))
        xt = _cross(xt, g[4], bf(x_wq[layer]), kx, vx, bf(x_wo[layer]), g[5])
        xt = _ffn(xt, g[6], g[7], *ffn2, layer)
    return xt.reshape(b, s, d)
```

```python
import functools

import jax
import jax.numpy as jnp
from jax import lax
from jax.experimental import pallas as pl
from jax.experimental.pallas import tpu as pltpu

F32 = jnp.float32
BF16 = jnp.bfloat16

EPS = 1e-6
SEQ = 4096
HEAD_DIM = 64
N_Q_HEADS = 16
N_KV_HEADS = 2
Q_PER_KV = N_Q_HEADS // N_KV_HEADS
ATTN_WIDTH = N_Q_HEADS * HEAD_DIM
BLOCK = 128
ROPE_DIM = HEAD_DIM // 4
ROPE_THETA = 500000.0
SGU_GROUPS = 8
SGU_WIDTH = 1024
POOL_WINDOWS = (2, 4, 8, 16)
POOL_HALO = 16
X_HEADS = 4
X_HEAD_DIM = 128
LANES = 128
FFN_SUB = 256
FFN_OUT = 512
FFN_ROWS = 16

VMEM_LIMIT_BYTES = 56 * 1024 * 1024


def _rms(x, g):
    return x * lax.rsqrt(jnp.mean(x * x, axis=-1, keepdims=True) + EPS) * g


def _params(*sem, vmem=VMEM_LIMIT_BYTES):
    return pltpu.CompilerParams(dimension_semantics=sem, vmem_limit_bytes=vmem)


def _ffn_body(x_ref, gpre_ref, gpost_ref, wg_ref, wu_ref, wd_ref, o_ref, h_ref):
    j = pl.program_id(1)
    tm, d = x_ref.shape

    def row_blocks(fn):
        for r in range(0, tm, FFN_ROWS):
            fn(slice(r, r + FFN_ROWS))

    @pl.when(j == 0)
    def _():
        def prologue(rows):
            h_ref[rows, :] = _rms(x_ref[rows, :], gpre_ref[...]).astype(BF16)
            o_ref[rows, :] = jnp.zeros((FFN_ROWS, d), F32)
        row_blocks(prologue)

    h = h_ref[...]
    fk = wg_ref.shape[1]
    for c0 in range(0, fk, FFN_SUB):
        gate = jnp.dot(h, wg_ref[:, c0:c0 + FFN_SUB], preferred_element_type=F32)
        up = jnp.dot(h, wu_ref[:, c0:c0 + FFN_SUB], preferred_element_type=F32)
        a = (jax.nn.silu(gate) * up).astype(BF16)
        for n0 in range(0, d, FFN_OUT):
            o_ref[:, n0:n0 + FFN_OUT] += jnp.dot(a, wd_ref[c0:c0 + FFN_SUB, n0:n0 + FFN_OUT],
                                                 preferred_element_type=F32)

    @pl.when(j == pl.num_programs(1) - 1)
    def _():
        def epilogue(rows):
            o_ref[rows, :] = x_ref[rows, :] + 0.5 * _rms(o_ref[rows, :], gpost_ref[...])
        row_blocks(epilogue)


def _ffn(x, g_pre, g_post, wg, wu, wd, layer, *, tm=1024, fk=512):
    t, d = x.shape
    f = wg.shape[2]
    return pl.pallas_call(
        _ffn_body,
        grid=(t // tm, f // fk),
        in_specs=[
            pl.BlockSpec((tm, d), lambda i, j: (i, 0)),
            pl.BlockSpec((1, d), lambda i, j: (0, 0)),
            pl.BlockSpec((1, d), lambda i, j: (0, 0)),
            pl.BlockSpec((None, d, fk), lambda i, j: (layer, 0, j)),
            pl.BlockSpec((None, d, fk), lambda i, j: (layer, 0, j)),
            pl.BlockSpec((None, fk, d), lambda i, j: (layer, j, 0)),
        ],
        out_specs=pl.BlockSpec((tm, d), lambda i, j: (i, 0)),
        out_shape=jax.ShapeDtypeStruct((t, d), F32),
        scratch_shapes=[pltpu.VMEM((tm, d), BF16)],
        compiler_params=_params("parallel", "arbitrary"),
        name="ffn",
    )(x, g_pre, g_post, wg, wu, wd)


def _rope(p, cos, sin_lo, sin_hi):
    return p * cos + pltpu.roll(p, 8, 1) * sin_hi + pltpu.roll(p, LANES - 8, 1) * sin_lo


def _dup_halves(p):
    lane = lax.broadcasted_iota(jnp.int32, p.shape, 1)
    lo = jnp.where(lane < HEAD_DIM, p, 0.0)
    hi = p - lo
    return lo + pltpu.roll(lo, HEAD_DIM, 1), hi + pltpu.roll(hi, HEAD_DIM, 1)


def _mixer_in_body(x_ref, g_ref, w_ref, cos_ref, slo_ref, shi_ref, q_ref, kv_ref, u_ref, v_ref, h_ref):
    h_ref[...] = _rms(x_ref[...], g_ref[...]).astype(BF16)
    h = h_ref[...]
    cos, slo, shi = cos_ref[...], slo_ref[...], shi_ref[...]

    def proj(c0, width):
        return jnp.dot(h, w_ref[:, c0:c0 + width], preferred_element_type=F32)

    chunk = 2 * LANES
    for c in range(ATTN_WIDTH // chunk):
        z = proj(c * chunk, chunk)
        for n in range(2):
            q_ref[:, c * chunk + n * LANES:c * chunk + (n + 1) * LANES] = (_rope(
                z[:, n * LANES:(n + 1) * LANES], cos, slo, shi) * (HEAD_DIM ** -0.5)).astype(BF16)
    z = proj(ATTN_WIDTH, chunk)
    k = _rope(z[:, :LANES], cos, slo, shi)
    v = z[:, LANES:]
    for n, part in enumerate(_dup_halves(k) + _dup_halves(v)):
        kv_ref[:, n * LANES:(n + 1) * LANES] = part.astype(BF16)
    c0 = ATTN_WIDTH + 2 * LANES
    for c in range(SGU_WIDTH // chunk):
        u_ref[:, c * chunk:(c + 1) * chunk] = jax.nn.gelu(proj(c0 + c * chunk, chunk)).astype(BF16)
        v_ref[:, c * chunk:(c + 1) * chunk] = jax.nn.gelu(proj(c0 + SGU_WIDTH + c * chunk, chunk)).astype(BF16)


def _mixer_in(x, g, w_in, cos, slo, shi, *, tm=256):
    t, d = x.shape
    n_in = w_in.shape[1]
    seq_tiles = SEQ // tm
    row = lambda i: (i, 0)
    pos = lambda i: (i % seq_tiles, 0)
    const = lambda i: (0, 0)
    return pl.pallas_call(
        _mixer_in_body,
        grid=(t // tm,),
        in_specs=[
            pl.BlockSpec((tm, d), row),
            pl.BlockSpec((1, d), const),
            pl.BlockSpec((d, n_in), const),
            pl.BlockSpec((tm, LANES), pos),
            pl.BlockSpec((tm, LANES), pos),
            pl.BlockSpec((tm, LANES), pos),
        ],
        out_specs=[
            pl.BlockSpec((tm, ATTN_WIDTH), row),
            pl.BlockSpec((tm, 4 * LANES), row),
            pl.BlockSpec((tm, SGU_WIDTH), row),
            pl.BlockSpec((tm, SGU_WIDTH), row),
        ],
        out_shape=[
            jax.ShapeDtypeStruct((t, ATTN_WIDTH), BF16),
            jax.ShapeDtypeStruct((t, 4 * LANES), BF16),
            jax.ShapeDtypeStruct((t, SGU_WIDTH), BF16),
            jax.ShapeDtypeStruct((t, SGU_WIDTH), BF16),
        ],
        scratch_shapes=[pltpu.VMEM((tm, d), BF16)],
        compiler_params=_params("parallel"),
        name="mixer_in",
    )(x, g, w_in, cos, slo, shi)


def _swa_block(q, kv_prev, kv_cur, sinks_ref, valid):
    lane = lax.broadcasted_iota(jnp.int32, (BLOCK, LANES), 1)
    lo_half = lane < HEAD_DIM
    zero = jnp.zeros((), BF16)
    outs = []
    for g in range(N_KV_HEADS):
        kd = jnp.concatenate([kv_prev[:, g * LANES:(g + 1) * LANES], kv_cur[:, g * LANES:(g + 1) * LANES]], axis=0)
        vd = jnp.concatenate([kv_prev[:, (2 + g) * LANES:(3 + g) * LANES],
                              kv_cur[:, (2 + g) * LANES:(3 + g) * LANES]], axis=0)
        parts = []
        for p in range(Q_PER_KV // 2):
            pair = q[:, (g * Q_PER_KV // 2 + p) * LANES:(g * Q_PER_KV // 2 + p + 1) * LANES]
            parts.append(jnp.where(lo_half, pair, zero))
            parts.append(jnp.where(lo_half, zero, pair))
        qs = jnp.concatenate(parts, axis=0)
        s_all = lax.dot_general(qs, kd, (((1,), (1,)), ((), ())), preferred_element_type=F32)
        es, inv = [], []
        for hq in range(Q_PER_KV):
            sink = sinks_ref[g * Q_PER_KV + hq]
            s = jnp.where(valid, s_all[hq * BLOCK:(hq + 1) * BLOCK], -1e30)
            m = jnp.maximum(jnp.max(s, axis=-1, keepdims=True), sink)
            e = jnp.exp(s - m)
            inv.append(1.0 / (jnp.sum(e, axis=-1, keepdims=True) + jnp.exp(sink - m)))
            es.append(e.astype(BF16))
        o_all = jnp.dot(jnp.concatenate(es, axis=0), vd, preferred_element_type=F32)
        for p in range(Q_PER_KV // 2):
            even = o_all[(2 * p) * BLOCK:(2 * p + 1) * BLOCK] * inv[2 * p]
            odd = o_all[(2 * p + 1) * BLOCK:(2 * p + 2) * BLOCK] * inv[2 * p + 1]
            outs.append(jnp.where(lo_half, even, odd).astype(BF16))
    return jnp.concatenate(outs, axis=1)


def _sgu_block(u, v, ln_g, ln_b, ws, bt_ref):
    vf = v.astype(F32)
    mu = jnp.mean(vf, axis=-1, keepdims=True)
    vc = vf - mu
    var = jnp.mean(vc * vc, axis=-1, keepdims=True)
    vn = (vc * lax.rsqrt(var + EPS) * ln_g + ln_b).astype(BF16)
    outs = []
    for g in range(SGU_GROUPS):
        mixed = jnp.dot(ws[g], vn[:, g * LANES:(g + 1) * LANES], preferred_element_type=F32) + bt_ref[:, g:g + 1]
        outs.append((u[:, g * LANES:(g + 1) * LANES].astype(F32) * mixed).astype(BF16))
    return jnp.concatenate(outs, axis=1)


def _mixer_out_body(sinks_ref, x_ref, q_ref, kv_ref, kvp_ref, u_ref, v_ref, lng_ref, lnb_ref, sw_ref, sbt_ref,
                    wout_ref, g_ref, o_ref, mix_ref):
    tm = x_ref.shape[0]
    nblk = tm // BLOCK
    blocks_per_seq = SEQ // BLOCK
    first_blk = pl.program_id(0) * nblk
    ri = lax.broadcasted_iota(jnp.int32, (BLOCK, BLOCK), 0)
    ci = lax.broadcasted_iota(jnp.int32, (BLOCK, BLOCK), 1)
    ws = [jnp.where(ci <= ri, sw_ref[g], 0.0).astype(BF16) for g in range(SGU_GROUPS)]
    qi = lax.broadcasted_iota(jnp.int32, (BLOCK, 2 * BLOCK), 0)
    kj = lax.broadcasted_iota(jnp.int32, (BLOCK, 2 * BLOCK), 1)
    rel = qi + BLOCK - kj
    band = (rel >= 0) & (rel < BLOCK)
    for b in range(nblk):
        r0 = b * BLOCK
        kv_cur = kv_ref[r0:r0 + BLOCK, :]
        kv_prev = kvp_ref[...] if b == 0 else kv_ref[r0 - BLOCK:r0, :]
        first_key = jnp.where(((first_blk + b) % blocks_per_seq) != 0, 0, BLOCK)
        valid = band & (kj >= first_key)
        mix_ref[r0:r0 + BLOCK, :ATTN_WIDTH] = _swa_block(q_ref[r0:r0 + BLOCK, :], kv_prev, kv_cur, sinks_ref, valid)
        mix_ref[r0:r0 + BLOCK, ATTN_WIDTH:] = _sgu_block(u_ref[r0:r0 + BLOCK, :], v_ref[r0:r0 + BLOCK, :],
                                                        lng_ref[...], lnb_ref[...], ws, sbt_ref)
    m = jnp.dot(mix_ref[...], wout_ref[...], preferred_element_type=F32)
    o_ref[...] = x_ref[...] + _rms(m, g_ref[...])


def _mixer_out(x, q, kv, u, v, sinks, ln_g, ln_b, sgu_w, sgu_bt, w_out, g, *, tm=256):
    t, d = x.shape
    nblk = tm // BLOCK
    row = lambda i: (i, 0)
    const = lambda i: (0, 0)
    return pl.pallas_call(
        _mixer_out_body,
        grid=(t // tm,),
        in_specs=[
            pl.BlockSpec(memory_space=pltpu.SMEM),
            pl.BlockSpec((tm, d), row),
            pl.BlockSpec((tm, ATTN_WIDTH), row),
            pl.BlockSpec((tm, 4 * LANES), row),
            pl.BlockSpec((BLOCK, 4 * LANES), lambda i: (jnp.maximum(i * nblk - 1, 0), 0)),
            pl.BlockSpec((tm, SGU_WIDTH), row),
            pl.BlockSpec((tm, SGU_WIDTH), row),
            pl.BlockSpec((1, SGU_WIDTH), const),
            pl.BlockSpec((1, SGU_WIDTH), const),
            pl.BlockSpec((SGU_GROUPS, BLOCK, BLOCK), lambda i: (0, 0, 0)),
            pl.BlockSpec((BLOCK, SGU_GROUPS), const),
            pl.BlockSpec((ATTN_WIDTH + SGU_WIDTH, d), const),
            pl.BlockSpec((1, d), const),
        ],
        out_specs=pl.BlockSpec((tm, d), row),
        out_shape=jax.ShapeDtypeStruct((t, d), F32),
        scratch_shapes=[pltpu.VMEM((tm, ATTN_WIDTH + SGU_WIDTH), BF16)],
        compiler_params=_params("parallel"),
        name="mixer_out",
    )(sinks, x, q, kv, kv, u, v, ln_g, ln_b, sgu_w, sgu_bt, w_out, g)


def _pool_body(x_ref, halo_ref, gpre_ref, w_ref, scale_ref, gpost_ref, o_ref):
    tm, d = x_ref.shape
    gd = d // len(POOL_WINDOWS)
    seq_start = (pl.program_id(0) * tm) % SEQ == 0
    x = x_ref[...]
    h = _rms(x, gpre_ref[...])
    h_halo = jnp.where(seq_start, 0.0, _rms(halo_ref[...], gpre_ref[...]))
    ext = jnp.concatenate([h_halo, h], axis=0)
    t_pos = (pl.program_id(0) * tm) % SEQ + lax.broadcasted_iota(jnp.int32, (tm, 1), 0)
    count = (t_pos + 1).astype(F32)
    ys = []
    for gi, w in enumerate(POOL_WINDOWS):
        s = ext[:, gi * gd:(gi + 1) * gd]
        span = 1
        while span < w:
            s = s + pltpu.roll(s, span, 0)
            span *= 2
        mean = s[POOL_HALO:] * (1.0 / jnp.minimum(count, float(w)))
        pooled = (mean - h[:, gi * gd:(gi + 1) * gd]).astype(BF16)
        ys.append(jnp.dot(pooled, w_ref[gi], preferred_element_type=F32))
    y = jnp.concatenate(ys, axis=1) * scale_ref[...]
    o_ref[...] = x + _rms(y, gpost_ref[...])


def _pool(x, g_pre, pool_w, pool_scale, g_post, *, tm=512):
    t, d = x.shape
    gd = d // len(POOL_WINDOWS)
    halo_blocks = tm // POOL_HALO
    row = lambda i: (i, 0)
    const = lambda i: (0, 0)
    return pl.pallas_call(
        _pool_body,
        grid=(t // tm,),
        in_specs=[
            pl.BlockSpec((tm, d), row),
            pl.BlockSpec((POOL_HALO, d), lambda i: (jnp.maximum(i * halo_blocks - 1, 0), 0)),
            pl.BlockSpec((1, d), const),
            pl.BlockSpec((len(POOL_WINDOWS), gd, gd), lambda i: (0, 0, 0)),
            pl.BlockSpec((1, d), const),
            pl.BlockSpec((1, d), const),
        ],
        out_specs=pl.BlockSpec((tm, d), row),
        out_shape=jax.ShapeDtypeStruct((t, d), F32),
        compiler_params=_params("parallel"),
        name="pool",
    )(x, x, g_pre, pool_w, pool_scale, g_post)


def _mem_kv_body(m_ref, g_ref, wk_ref, wv_ref, k_ref, v_ref):
    h = _rms(m_ref[...], g_ref[...]).astype(BF16)
    k_ref[...] = jnp.dot(h, wk_ref[...], preferred_element_type=F32).astype(BF16)
    v_ref[...] = jnp.dot(h, wv_ref[...], preferred_element_type=F32).astype(BF16)


def _mem_kv(mem, g, wk, wv, *, tm=256):
    t, d = mem.shape
    n = wk.shape[1]
    row = lambda i: (i, 0)
    const = lambda i: (0, 0)
    return pl.pallas_call(
        _mem_kv_body,
        grid=(t // tm,),
        in_specs=[pl.BlockSpec((tm, d), row), pl.BlockSpec((1, d), const),
                  pl.BlockSpec((d, n), const), pl.BlockSpec((d, n), const)],
        out_specs=[pl.BlockSpec((tm, n), row), pl.BlockSpec((tm, n), row)],
        out_shape=[jax.ShapeDtypeStruct((t, n), BF16), jax.ShapeDtypeStruct((t, n), BF16)],
        compiler_params=_params("parallel"),
        name="mem_kv",
    )(mem, g, wk, wv)


def _cross_body(x_ref, gpre_ref, wq_ref, k_ref, v_ref, wo_ref, gpost_ref, o_ref):
    x = x_ref[...]
    h = _rms(x, gpre_ref[...]).astype(BF16)
    q = jnp.dot(h, wq_ref[...], preferred_element_type=F32).astype(BF16)
    outs = []
    for hh in range(X_HEADS):
        sl = slice(hh * X_HEAD_DIM, (hh + 1) * X_HEAD_DIM)
        s = lax.dot_general(q[:, sl], k_ref[:, sl], (((1,), (1,)), ((), ())),
                            preferred_element_type=F32) * (X_HEAD_DIM ** -0.5)
        e = jnp.exp(s - jnp.max(s, axis=-1, keepdims=True))
        denom = jnp.sum(e, axis=-1, keepdims=True)
        o = jnp.dot(e.astype(BF16), v_ref[:, sl], preferred_element_type=F32) * (1.0 / denom)
        outs.append(o.astype(BF16))
    y = jnp.dot(jnp.concatenate(outs, axis=1), wo_ref[...], preferred_element_type=F32)
    o_ref[...] = x + _rms(y, gpost_ref[...])


def _cross(x, g_pre, wq, k, v, wo, g_post, *, tm=512):
    t, d = x.shape
    n = wq.shape[1]
    mem_len = k.shape[0] // (t // SEQ)
    tiles_per_seq = SEQ // tm
    row = lambda i: (i, 0)
    const = lambda i: (0, 0)
    batch = lambda i: (i // tiles_per_seq, 0)
    return pl.pallas_call(
        _cross_body,
        grid=(t // tm,),
        in_specs=[
            pl.BlockSpec((tm, d), row),
            pl.BlockSpec((1, d), const),
            pl.BlockSpec((d, n), const),
            pl.BlockSpec((mem_len, n), batch),
            pl.BlockSpec((mem_len, n), batch),
            pl.BlockSpec((n, d), const),
            pl.BlockSpec((1, d), const),
        ],
        out_specs=pl.BlockSpec((tm, d), row),
        out_shape=jax.ShapeDtypeStruct((t, d), F32),
        compiler_params=_params("parallel"),
        name="cross",
    )(x, g_pre, wq, k, v, wo, g_post)


def _rope_tables():
    half = ROPE_DIM // 2
    inv = ROPE_THETA ** (-jnp.arange(half, dtype=F32) * 2.0 / ROPE_DIM)
    ang = jnp.arange(SEQ, dtype=F32)[:, None] * inv[None, :]
    cos, sin = jnp.cos(ang), jnp.sin(ang)
    ones = jnp.ones((SEQ, HEAD_DIM - ROPE_DIM), F32)
    zeros = jnp.zeros((SEQ, HEAD_DIM - ROPE_DIM), F32)
    z8 = jnp.zeros((SEQ, half), F32)
    cos_h = jnp.concatenate([cos, cos, ones], axis=1)
    slo_h = jnp.concatenate([-sin, z8, zeros], axis=1)
    shi_h = jnp.concatenate([z8, sin, zeros], axis=1)
    two = lambda a: jnp.concatenate([a, a], axis=1)
    return two(cos_h), two(slo_h), two(shi_h)


def kernel(x, mem, norms, mem_norm, ffn1_wg, ffn1_wu, ffn1_wd, ffn2_wg, ffn2_wu, ffn2_wd, x_wq, x_wk, x_wv, x_wo,
           mix_w_in, mix_w_out, attn_sinks, sgu_ln_g, sgu_ln_b, sgu_w, sgu_b, pool_w, pool_scale):
    b, s, d = x.shape
    depth = norms.shape[0]
    assert s == SEQ
    bf = lambda a: a.astype(BF16)
    xt = x.reshape(b * s, d)
    memt = mem.reshape(b * mem.shape[1], d)
    cos, slo, shi = _rope_tables()
    ffn1 = (bf(ffn1_wg), bf(ffn1_wu), bf(ffn1_wd))
    ffn2 = (bf(ffn2_wg), bf(ffn2_wu), bf(ffn2_wd))
    for layer in range(depth):
        g = norms[layer][:, None, :]
        xt = _ffn(xt, g[0], g[1], *ffn1, layer)
        i = layer // 2
        if layer % 2 == 0:
            q, kv, u, v = _mixer_in(xt, g[2], bf(mix_w_in[i]), cos, slo, shi)
            xt = _mixer_out(xt, q, kv, u, v, attn_sinks[i], sgu_ln_g[i][None], sgu_ln_b[i][None], sgu_w[i],
                            sgu_b[i].T, bf(mix_w_out[i]), g[3])
        else:
            xt = _pool(xt, g[2], bf(pool_w[i]), pool_scale[i][None], g[3])
        kx, vx = _mem_kv(memt, mem_norm[layer][None], bf(x_wk[layer]), bf(x_wv[layer]))
        xt = _cross(xt, g[4], bf(x_wq[layer]), kx, vx, bf(x_wo[layer]), g[5])
        xt = _ffn(xt, g[6], g[7], *ffn2, layer)
    return xt.reshape(b, s, d)
```

```python
import functools

import jax
import jax.numpy as jnp
from jax import lax
from jax.experimental import pallas as pl
from jax.experimental.pallas import tpu as pltpu

F32 = jnp.float32
BF16 = jnp.bfloat16

EPS = 1e-6
SEQ = 4096
HEAD_DIM = 64
N_Q_HEADS = 16
N_KV_HEADS = 2
Q_PER_KV = N_Q_HEADS // N_KV_HEADS
ATTN_WIDTH = N_Q_HEADS * HEAD_DIM
BLOCK = 128
ROPE_DIM = HEAD_DIM // 4
ROPE_THETA = 500000.0
SGU_GROUPS = 8
SGU_WIDTH = 1024
POOL_WINDOWS = (2, 4, 8, 16)
POOL_HALO = 16
X_HEADS = 4
X_HEAD_DIM = 128
LANES = 128
ROW_SUB = 256
FFN_SUB = 256
FFN_OUT = 512
FFN_ROWS = 16

VMEM_LIMIT_BYTES = 56 * 1024 * 1024


def _rms(x, g):
    return x * lax.rsqrt(jnp.mean(x * x, axis=-1, keepdims=True) + EPS) * g


def _params(*sem, vmem=VMEM_LIMIT_BYTES):
    return pltpu.CompilerParams(dimension_semantics=sem, vmem_limit_bytes=vmem)


def _ffn_body(x_ref, gpre_ref, gpost_ref, wg_ref, wu_ref, wd_ref, o_ref, h_ref):
    j = pl.program_id(1)
    last = pl.num_programs(1) - 1
    tm, d = x_ref.shape
    fk = wg_ref.shape[1]

    def prologue(r0):
        for r in range(r0, r0 + ROW_SUB, FFN_ROWS):
            rows = slice(r, r + FFN_ROWS)
            h_ref[rows, :] = _rms(x_ref[rows, :], gpre_ref[...]).astype(BF16)
            o_ref[rows, :] = jnp.zeros((FFN_ROWS, d), F32)

    def epilogue(r0):
        for r in range(r0, r0 + ROW_SUB, FFN_ROWS):
            rows = slice(r, r + FFN_ROWS)
            o_ref[rows, :] = x_ref[rows, :] + 0.5 * _rms(o_ref[rows, :], gpost_ref[...])

    def chunk(rows, cols):
        h = h_ref[rows, :]
        gate = jnp.dot(h, wg_ref[:, cols], preferred_element_type=F32)
        up = jnp.dot(h, wu_ref[:, cols], preferred_element_type=F32)
        a = (jax.nn.silu(gate) * up).astype(BF16)
        for n0 in range(0, d, FFN_OUT):
            o_ref[rows, n0:n0 + FFN_OUT] += jnp.dot(a, wd_ref[cols, n0:n0 + FFN_OUT], preferred_element_type=F32)

    @pl.when(j == 0)
    def _():
        for r0 in range(0, tm, ROW_SUB):
            prologue(r0)
            chunk(slice(r0, r0 + ROW_SUB), slice(0, fk))

    @pl.when((j > 0) & (j < last))
    def _():
        for c0 in range(0, fk, FFN_SUB):
            chunk(slice(0, tm), slice(c0, c0 + FFN_SUB))

    @pl.when(j == last)
    def _():
        for r0 in range(0, tm, ROW_SUB):
            chunk(slice(r0, r0 + ROW_SUB), slice(0, fk))
            epilogue(r0)


def _ffn(x, g_pre, g_post, wg, wu, wd, layer, *, tm=1024, fk=512):
    t, d = x.shape
    f = wg.shape[2]
    return pl.pallas_call(
        _ffn_body,
        grid=(t // tm, f // fk),
        in_specs=[
            pl.BlockSpec((tm, d), lambda i, j: (i, 0)),
            pl.BlockSpec((1, d), lambda i, j: (0, 0)),
            pl.BlockSpec((1, d), lambda i, j: (0, 0)),
            pl.BlockSpec((None, d, fk), lambda i, j: (layer, 0, j)),
            pl.BlockSpec((None, d, fk), lambda i, j: (layer, 0, j)),
            pl.BlockSpec((None, fk, d), lambda i, j: (layer, j, 0)),
        ],
        out_specs=pl.BlockSpec((tm, d), lambda i, j: (i, 0)),
        out_shape=jax.ShapeDtypeStruct((t, d), F32),
        scratch_shapes=[pltpu.VMEM((tm, d), BF16)],
        compiler_params=_params("parallel", "arbitrary"),
        name="ffn",
    )(x, g_pre, g_post, wg, wu, wd)


def _rope(p, cos, sin_lo, sin_hi):
    return p * cos + pltpu.roll(p, 8, 1) * sin_hi + pltpu.roll(p, LANES - 8, 1) * sin_lo


def _dup_halves(p):
    lane = lax.broadcasted_iota(jnp.int32, p.shape, 1)
    lo = jnp.where(lane < HEAD_DIM, p, 0.0)
    hi = p - lo
    return lo + pltpu.roll(lo, HEAD_DIM, 1), hi + pltpu.roll(hi, HEAD_DIM, 1)


def _mixer_in_rows(rows, x_ref, g_ref, w_ref, cos_ref, slo_ref, shi_ref, q_ref, kv_ref, u_ref, v_ref):
    h = _rms(x_ref[rows, :], g_ref[...]).astype(BF16)
    cos, slo, shi = cos_ref[rows, :], slo_ref[rows, :], shi_ref[rows, :]

    def proj(c0, width):
        return jnp.dot(h, w_ref[:, c0:c0 + width], preferred_element_type=F32)

    chunk = 2 * LANES
    for c in range(ATTN_WIDTH // chunk):
        z = proj(c * chunk, chunk)
        for n in range(2):
            q_ref[rows, c * chunk + n * LANES:c * chunk + (n + 1) * LANES] = (_rope(
                z[:, n * LANES:(n + 1) * LANES], cos, slo, shi) * (HEAD_DIM ** -0.5)).astype(BF16)
    z = proj(ATTN_WIDTH, chunk)
    k = _rope(z[:, :LANES], cos, slo, shi)
    v = z[:, LANES:]
    for n, part in enumerate(_dup_halves(k) + _dup_halves(v)):
        kv_ref[rows, n * LANES:(n + 1) * LANES] = part.astype(BF16)
    c0 = ATTN_WIDTH + 2 * LANES
    for c in range(SGU_WIDTH // chunk):
        u_ref[rows, c * chunk:(c + 1) * chunk] = jax.nn.gelu(proj(c0 + c * chunk, chunk)).astype(BF16)
        v_ref[rows, c * chunk:(c + 1) * chunk] = jax.nn.gelu(proj(c0 + SGU_WIDTH + c * chunk, chunk)).astype(BF16)


def _mixer_in_body(x_ref, *refs):
    for r0 in range(0, x_ref.shape[0], ROW_SUB):
        _mixer_in_rows(slice(r0, r0 + ROW_SUB), x_ref, *refs)


def _mixer_in(x, g, w_in, cos, slo, shi, *, tm=512):
    t, d = x.shape
    n_in = w_in.shape[1]
    seq_tiles = SEQ // tm
    row = lambda i: (i, 0)
    pos = lambda i: (i % seq_tiles, 0)
    const = lambda i: (0, 0)
    return pl.pallas_call(
        _mixer_in_body,
        grid=(t // tm,),
        in_specs=[
            pl.BlockSpec((tm, d), row),
            pl.BlockSpec((1, d), const),
            pl.BlockSpec((d, n_in), const),
            pl.BlockSpec((tm, LANES), pos),
            pl.BlockSpec((tm, LANES), pos),
            pl.BlockSpec((tm, LANES), pos),
        ],
        out_specs=[
            pl.BlockSpec((tm, ATTN_WIDTH), row),
            pl.BlockSpec((tm, 4 * LANES), row),
            pl.BlockSpec((tm, SGU_WIDTH), row),
            pl.BlockSpec((tm, SGU_WIDTH), row),
        ],
        out_shape=[
            jax.ShapeDtypeStruct((t, ATTN_WIDTH), BF16),
            jax.ShapeDtypeStruct((t, 4 * LANES), BF16),
            jax.ShapeDtypeStruct((t, SGU_WIDTH), BF16),
            jax.ShapeDtypeStruct((t, SGU_WIDTH), BF16),
        ],
        compiler_params=_params("parallel"),
        name="mixer_in",
    )(x, g, w_in, cos, slo, shi)


def _swa_block(q, kv_prev, kv_cur, sinks_ref, valid):
    lane = lax.broadcasted_iota(jnp.int32, (BLOCK, LANES), 1)
    lo_half = lane < HEAD_DIM
    zero = jnp.zeros((), BF16)
    outs = []
    for g in range(N_KV_HEADS):
        kd = jnp.concatenate([kv_prev[:, g * LANES:(g + 1) * LANES], kv_cur[:, g * LANES:(g + 1) * LANES]], axis=0)
        vd = jnp.concatenate([kv_prev[:, (2 + g) * LANES:(3 + g) * LANES],
                              kv_cur[:, (2 + g) * LANES:(3 + g) * LANES]], axis=0)
        parts = []
        for p in range(Q_PER_KV // 2):
            pair = q[:, (g * Q_PER_KV // 2 + p) * LANES:(g * Q_PER_KV // 2 + p + 1) * LANES]
            parts.append(jnp.where(lo_half, pair, zero))
            parts.append(jnp.where(lo_half, zero, pair))
        qs = jnp.concatenate(parts, axis=0)
        s_all = lax.dot_general(qs, kd, (((1,), (1,)), ((), ())), preferred_element_type=F32)
        es, inv = [], []
        for hq in range(Q_PER_KV):
            sink = sinks_ref[g * Q_PER_KV + hq]
            s = jnp.where(valid, s_all[hq * BLOCK:(hq + 1) * BLOCK], -1e30)
            m = jnp.maximum(jnp.max(s, axis=-1, keepdims=True), sink)
            e = jnp.exp(s - m)
            inv.append(1.0 / (jnp.sum(e, axis=-1, keepdims=True) + jnp.exp(sink - m)))
            es.append(e.astype(BF16))
        o_all = jnp.dot(jnp.concatenate(es, axis=0), vd, preferred_element_type=F32)
        for p in range(Q_PER_KV // 2):
            even = o_all[(2 * p) * BLOCK:(2 * p + 1) * BLOCK] * inv[2 * p]
            odd = o_all[(2 * p + 1) * BLOCK:(2 * p + 2) * BLOCK] * inv[2 * p + 1]
            outs.append(jnp.where(lo_half, even, odd).astype(BF16))
    return jnp.concatenate(outs, axis=1)


def _sgu_block(u, v, ln_g, ln_b, ws, bt_ref):
    vf = v.astype(F32)
    mu = jnp.mean(vf, axis=-1, keepdims=True)
    vc = vf - mu
    var = jnp.mean(vc * vc, axis=-1, keepdims=True)
    vn = (vc * lax.rsqrt(var + EPS) * ln_g + ln_b).astype(BF16)
    outs = []
    for g in range(SGU_GROUPS):
        mixed = jnp.dot(ws[g], vn[:, g * LANES:(g + 1) * LANES], preferred_element_type=F32) + bt_ref[:, g:g + 1]
        outs.append((u[:, g * LANES:(g + 1) * LANES].astype(F32) * mixed).astype(BF16))
    return jnp.concatenate(outs, axis=1)


def _mixer_out_body(sinks_ref, x_ref, q_ref, kv_ref, kvp_ref, u_ref, v_ref, lng_ref, lnb_ref, sw_ref, sbt_ref,
                    wout_ref, g_ref, o_ref, mix_ref):
    tm = x_ref.shape[0]
    nblk = tm // BLOCK
    blocks_per_seq = SEQ // BLOCK
    first_blk = pl.program_id(0) * nblk
    ri = lax.broadcasted_iota(jnp.int32, (BLOCK, BLOCK), 0)
    ci = lax.broadcasted_iota(jnp.int32, (BLOCK, BLOCK), 1)
    ws = [jnp.where(ci <= ri, sw_ref[g], 0.0).astype(BF16) for g in range(SGU_GROUPS)]
    qi = lax.broadcasted_iota(jnp.int32, (BLOCK, 2 * BLOCK), 0)
    kj = lax.broadcasted_iota(jnp.int32, (BLOCK, 2 * BLOCK), 1)
    rel = qi + BLOCK - kj
    band = (rel >= 0) & (rel < BLOCK)
    for b in range(nblk):
        r0 = b * BLOCK
        kv_cur = kv_ref[r0:r0 + BLOCK, :]
        kv_prev = kvp_ref[...] if b == 0 else kv_ref[r0 - BLOCK:r0, :]
        first_key = jnp.where(((first_blk + b) % blocks_per_seq) != 0, 0, BLOCK)
        valid = band & (kj >= first_key)
        mix_ref[r0:r0 + BLOCK, :ATTN_WIDTH] = _swa_block(q_ref[r0:r0 + BLOCK, :], kv_prev, kv_cur, sinks_ref, valid)
        mix_ref[r0:r0 + BLOCK, ATTN_WIDTH:] = _sgu_block(u_ref[r0:r0 + BLOCK, :], v_ref[r0:r0 + BLOCK, :],
                                                        lng_ref[...], lnb_ref[...], ws, sbt_ref)
    m = jnp.dot(mix_ref[...], wout_ref[...], preferred_element_type=F32)
    o_ref[...] = x_ref[...] + _rms(m, g_ref[...])


def _mixer_out(x, q, kv, u, v, sinks, ln_g, ln_b, sgu_w, sgu_bt, w_out, g, *, tm=512):
    t, d = x.shape
    nblk = tm // BLOCK
    row = lambda i: (i, 0)
    const = lambda i: (0, 0)
    return pl.pallas_call(
        _mixer_out_body,
        grid=(t // tm,),
        in_specs=[
            pl.BlockSpec(memory_space=pltpu.SMEM),
            pl.BlockSpec((tm, d), row),
            pl.BlockSpec((tm, ATTN_WIDTH), row),
            pl.BlockSpec((tm, 4 * LANES), row),
            pl.BlockSpec((BLOCK, 4 * LANES), lambda i: (jnp.maximum(i * nblk - 1, 0), 0)),
            pl.BlockSpec((tm, SGU_WIDTH), row),
            pl.BlockSpec((tm, SGU_WIDTH), row),
            pl.BlockSpec((1, SGU_WIDTH), const),
            pl.BlockSpec((1, SGU_WIDTH), const),
            pl.BlockSpec((SGU_GROUPS, BLOCK, BLOCK), lambda i: (0, 0, 0)),
            pl.BlockSpec((BLOCK, SGU_GROUPS), const),
            pl.BlockSpec((ATTN_WIDTH + SGU_WIDTH, d), const),
            pl.BlockSpec((1, d), const),
        ],
        out_specs=pl.BlockSpec((tm, d), row),
        out_shape=jax.ShapeDtypeStruct((t, d), F32),
        scratch_shapes=[pltpu.VMEM((tm, ATTN_WIDTH + SGU_WIDTH), BF16)],
        compiler_params=_params("parallel"),
        name="mixer_out",
    )(sinks, x, q, kv, kv, u, v, ln_g, ln_b, sgu_w, sgu_bt, w_out, g)


def _pool_body(x_ref, halo_ref, gpre_ref, w_ref, scale_ref, gpost_ref, o_ref):
    tm, d = x_ref.shape
    gd = d // len(POOL_WINDOWS)
    seq_start = (pl.program_id(0) * tm) % SEQ == 0
    x = x_ref[...]
    h = _rms(x, gpre_ref[...])
    h_halo = jnp.where(seq_start, 0.0, _rms(halo_ref[...], gpre_ref[...]))
    ext = jnp.concatenate([h_halo, h], axis=0)
    t_pos = (pl.program_id(0) * tm) % SEQ + lax.broadcasted_iota(jnp.int32, (tm, 1), 0)
    count = (t_pos + 1).astype(F32)
    ys = []
    for gi, w in enumerate(POOL_WINDOWS):
        s = ext[:, gi * gd:(gi + 1) * gd]
        span = 1
        while span < w:
            s = s + pltpu.roll(s, span, 0)
            span *= 2
        mean = s[POOL_HALO:] * (1.0 / jnp.minimum(count, float(w)))
        pooled = (mean - h[:, gi * gd:(gi + 1) * gd]).astype(BF16)
        ys.append(jnp.dot(pooled, w_ref[gi], preferred_element_type=F32))
    y = jnp.concatenate(ys, axis=1) * scale_ref[...]
    o_ref[...] = x + _rms(y, gpost_ref[...])


def _pool(x, g_pre, pool_w, pool_scale, g_post, *, tm=1024):
    t, d = x.shape
    gd = d // len(POOL_WINDOWS)
    halo_blocks = tm // POOL_HALO
    row = lambda i: (i, 0)
    const = lambda i: (0, 0)
    return pl.pallas_call(
        _pool_body,
        grid=(t // tm,),
        in_specs=[
            pl.BlockSpec((tm, d), row),
            pl.BlockSpec((POOL_HALO, d), lambda i: (jnp.maximum(i * halo_blocks - 1, 0), 0)),
            pl.BlockSpec((1, d), const),
            pl.BlockSpec((len(POOL_WINDOWS), gd, gd), lambda i: (0, 0, 0)),
            pl.BlockSpec((1, d), const),
            pl.BlockSpec((1, d), const),
        ],
        out_specs=pl.BlockSpec((tm, d), row),
        out_shape=jax.ShapeDtypeStruct((t, d), F32),
        compiler_params=_params("parallel"),
        name="pool",
    )(x, x, g_pre, pool_w, pool_scale, g_post)


def _mem_kv_body(m_ref, g_ref, wk_ref, wv_ref, k_ref, v_ref):
    h = _rms(m_ref[...], g_ref[...]).astype(BF16)
    k_ref[...] = jnp.dot(h, wk_ref[...], preferred_element_type=F32).astype(BF16)
    v_ref[...] = jnp.dot(h, wv_ref[...], preferred_element_type=F32).astype(BF16)


def _mem_kv(mem, g, wk, wv, *, tm=256):
    t, d = mem.shape
    n = wk.shape[1]
    row = lambda i: (i, 0)
    const = lambda i: (0, 0)
    return pl.pallas_call(
        _mem_kv_body,
        grid=(t // tm,),
        in_specs=[pl.BlockSpec((tm, d), row), pl.BlockSpec((1, d), const),
                  pl.BlockSpec((d, n), const), pl.BlockSpec((d, n), const)],
        out_specs=[pl.BlockSpec((tm, n), row), pl.BlockSpec((tm, n), row)],
        out_shape=[jax.ShapeDtypeStruct((t, n), BF16), jax.ShapeDtypeStruct((t, n), BF16)],
        compiler_params=_params("parallel"),
        name="mem_kv",
    )(mem, g, wk, wv)


def _cross_body(x_ref, gpre_ref, wq_ref, k_ref, v_ref, wo_ref, gpost_ref, o_ref):
    x = x_ref[...]
    h = _rms(x, gpre_ref[...]).astype(BF16)
    q = jnp.dot(h, wq_ref[...], preferred_element_type=F32).astype(BF16)
    outs = []
    for hh in range(X_HEADS):
        sl = slice(hh * X_HEAD_DIM, (hh + 1) * X_HEAD_DIM)
        s = lax.dot_general(q[:, sl], k_ref[:, sl], (((1,), (1,)), ((), ())),
                            preferred_element_type=F32) * (X_HEAD_DIM ** -0.5)
        e = jnp.exp(s - jnp.max(s, axis=-1, keepdims=True))
        denom = jnp.sum(e, axis=-1, keepdims=True)
        o = jnp.dot(e.astype(BF16), v_ref[:, sl], preferred_element_type=F32) * (1.0 / denom)
        outs.append(o.astype(BF16))
    y = jnp.dot(jnp.concatenate(outs, axis=1), wo_ref[...], preferred_element_type=F32)
    o_ref[...] = x + _rms(y, gpost_ref[...])


def _cross(x, g_pre, wq, k, v, wo, g_post, *, tm=1024):
    t, d = x.shape
    n = wq.shape[1]
    mem_len = k.shape[0] // (t // SEQ)
    tiles_per_seq = SEQ // tm
    row = lambda i: (i, 0)
    const = lambda i: (0, 0)
    batch = lambda i: (i // tiles_per_seq, 0)
    return pl.pallas_call(
        _cross_body,
        grid=(t // tm,),
        in_specs=[
            pl.BlockSpec((tm, d), row),
            pl.BlockSpec((1, d), const),
            pl.BlockSpec((d, n), const),
            pl.BlockSpec((mem_len, n), batch),
            pl.BlockSpec((mem_len, n), batch),
            pl.BlockSpec((n, d), const),
            pl.BlockSpec((1, d), const),
        ],
        out_specs=pl.BlockSpec((tm, d), row),
        out_shape=jax.ShapeDtypeStruct((t, d), F32),
        compiler_params=_params("parallel"),
        name="cross",
    )(x, g_pre, wq, k, v, wo, g_post)


def _rope_tables():
    half = ROPE_DIM // 2
    inv = ROPE_THETA ** (-jnp.arange(half, dtype=F32) * 2.0 / ROPE_DIM)
    ang = jnp.arange(SEQ, dtype=F32)[:, None] * inv[None, :]
    cos, sin = jnp.cos(ang), jnp.sin(ang)
    ones = jnp.ones((SEQ, HEAD_DIM - ROPE_DIM), F32)
    zeros = jnp.zeros((SEQ, HEAD_DIM - ROPE_DIM), F32)
    z8 = jnp.zeros((SEQ, half), F32)
    cos_h = jnp.concatenate([cos, cos, ones], axis=1)
    slo_h = jnp.concatenate([-sin, z8, zeros], axis=1)
    shi_h = jnp.concatenate([z8, sin, zeros], axis=1)
    two = lambda a: jnp.concatenate([a, a], axis=1)
    return two(cos_h), two(slo_h), two(shi_h)


def kernel(x, mem, norms, mem_norm, ffn1_wg, ffn1_wu, ffn1_wd, ffn2_wg, ffn2_wu, ffn2_wd, x_wq, x_wk, x_wv, x_wo,
           mix_w_in, mix_w_out, attn_sinks, sgu_ln_g, sgu_ln_b, sgu_w, sgu_b, pool_w, pool_scale):
    b, s, d = x.shape
    depth = norms.shape[0]
    assert s == SEQ
    bf = lambda a: a.astype(BF16)
    xt = x.reshape(b * s, d)
    memt = mem.reshape(b * mem.shape[1], d)
    cos, slo, shi = _rope_tables()
    ffn1 = (bf(ffn1_wg), bf(ffn1_wu), bf(ffn1_wd))
    ffn2 = (bf(ffn2_wg), bf(ffn2_wu), bf(ffn2_wd))
    for layer in range(depth):
        g = norms[layer][:, None, :]
        xt = _ffn(xt, g[0], g[1], *ffn1, layer)
        i = layer // 2
        if layer % 2 == 0:
            q, kv, u, v = _mixer_in(xt, g[2], bf(mix_w_in[i]), cos, slo, shi)
            xt = _mixer_out(xt, q, kv, u, v, attn_sinks[i], sgu_ln_g[i][None], sgu_ln_b[i][None], sgu_w[i],
                            sgu_b[i].T, bf(mix_w_out[i]), g[3])
        else:
            xt = _pool(xt, g[2], bf(pool_w[i]), pool_scale[i][None], g[3])
        kx, vx = _mem_kv(memt, mem_norm[layer][None], bf(x_wk[layer]), bf(x_wv[layer]))
        xt = _cross(xt, g[4], bf(x_wq[layer]), kx, vx, bf(x_wo[layer]), g[5])
        xt = _ffn(xt, g[6], g[7], *ffn2, layer)
    return xt.reshape(b, s, d)
```

```python
import functools

import jax
import jax.numpy as jnp
from jax import lax
from jax.experimental import pallas as pl
from jax.experimental.pallas import tpu as pltpu

F32 = jnp.float32
BF16 = jnp.bfloat16

EPS = 1e-6
SEQ = 4096
HEAD_DIM = 64
N_Q_HEADS = 16
N_KV_HEADS = 2
Q_PER_KV = N_Q_HEADS // N_KV_HEADS
ATTN_WIDTH = N_Q_HEADS * HEAD_DIM
BLOCK = 128
ROPE_DIM = HEAD_DIM // 4
ROPE_THETA = 500000.0
SGU_GROUPS = 8
SGU_WIDTH = 1024
POOL_WINDOWS = (2, 4, 8, 16)
POOL_HALO = 16
X_HEADS = 4
X_HEAD_DIM = 128
LANES = 128
ROW_SUB = 256
FFN_SUB = 256
FFN_OUT = 512
CAST_COLS = 4096
BF16_SUBLANES = 16
FFN_ROWS = 16

VMEM_LIMIT_BYTES = 56 * 1024 * 1024


def _rms(x, g):
    return x * lax.rsqrt(jnp.mean(x * x, axis=-1, keepdims=True) + EPS) * g


def _params(*sem, vmem=VMEM_LIMIT_BYTES):
    return pltpu.CompilerParams(dimension_semantics=sem, vmem_limit_bytes=vmem)


def _ffn_body(n_cast, x_ref, gpre_ref, gpost_ref, wg_ref, wu_ref, wd_ref, *refs):
    src_refs, o_ref, dst_refs, h_ref = refs[:n_cast], refs[n_cast], refs[n_cast + 1:-1], refs[-1]
    j = pl.program_id(1)
    last = pl.num_programs(1) - 1
    tm, d = x_ref.shape
    fk = wg_ref.shape[1]
    for src_ref, dst_ref in zip(src_refs, dst_refs):
        dst_ref[...] = src_ref[...].astype(BF16)

    def prologue(r0):
        for r in range(r0, r0 + ROW_SUB, FFN_ROWS):
            rows = slice(r, r + FFN_ROWS)
            h_ref[rows, :] = _rms(x_ref[rows, :], gpre_ref[...]).astype(BF16)
            o_ref[rows, :] = jnp.zeros((FFN_ROWS, d), F32)

    def epilogue(r0):
        for r in range(r0, r0 + ROW_SUB, FFN_ROWS):
            rows = slice(r, r + FFN_ROWS)
            o_ref[rows, :] = x_ref[rows, :] + 0.5 * _rms(o_ref[rows, :], gpost_ref[...])

    def chunk(rows, cols):
        h = h_ref[rows, :]
        gate = jnp.dot(h, wg_ref[:, cols], preferred_element_type=F32)
        up = jnp.dot(h, wu_ref[:, cols], preferred_element_type=F32)
        a = (jax.nn.silu(gate) * up).astype(BF16)
        for n0 in range(0, d, FFN_OUT):
            o_ref[rows, n0:n0 + FFN_OUT] += jnp.dot(a, wd_ref[cols, n0:n0 + FFN_OUT], preferred_element_type=F32)

    @pl.when(j == 0)
    def _():
        for r0 in range(0, tm, ROW_SUB):
            prologue(r0)
            chunk(slice(r0, r0 + ROW_SUB), slice(0, fk))

    @pl.when((j > 0) & (j < last))
    def _():
        for c0 in range(0, fk, FFN_SUB):
            chunk(slice(0, tm), slice(c0, c0 + FFN_SUB))

    @pl.when(j == last)
    def _():
        for r0 in range(0, tm, ROW_SUB):
            chunk(slice(r0, r0 + ROW_SUB), slice(0, fk))
            epilogue(r0)


def _ffn(x, g_pre, g_post, wg, wu, wd, cast_srcs=(), cast_layer=0, *, tm=1024, fk=512):
    t, d = x.shape
    f = wg.shape[1]
    steps = (t // tm) * (f // fk)
    srcs, dst_shapes = [], []
    for w in cast_srcs:
        rows = w.shape[1] * w.shape[2] // CAST_COLS
        assert rows * CAST_COLS == w.shape[1] * w.shape[2] and rows % (steps * BF16_SUBLANES) == 0
        srcs.append(w.reshape(w.shape[0], rows, CAST_COLS))
        dst_shapes.append(jax.ShapeDtypeStruct((rows, CAST_COLS), BF16))
    n_cast = len(srcs)
    step = lambda i, j: i * (f // fk) + j
    outs = pl.pallas_call(
        functools.partial(_ffn_body, n_cast),
        grid=(t // tm, f // fk),
        in_specs=[
            pl.BlockSpec((tm, d), lambda i, j: (i, 0)),
            pl.BlockSpec((1, d), lambda i, j: (0, 0)),
            pl.BlockSpec((1, d), lambda i, j: (0, 0)),
            pl.BlockSpec((d, fk), lambda i, j: (0, j)),
            pl.BlockSpec((d, fk), lambda i, j: (0, j)),
            pl.BlockSpec((fk, d), lambda i, j: (j, 0)),
        ] + [pl.BlockSpec((None, s.shape[1] // steps, CAST_COLS), lambda i, j: (cast_layer, step(i, j), 0))
             for s in srcs],
        out_specs=[pl.BlockSpec((tm, d), lambda i, j: (i, 0))]
        + [pl.BlockSpec((s.shape[0] // steps, CAST_COLS), lambda i, j: (step(i, j), 0)) for s in dst_shapes],
        out_shape=[jax.ShapeDtypeStruct((t, d), F32)] + dst_shapes,
        scratch_shapes=[pltpu.VMEM((tm, d), BF16)],
        compiler_params=_params("arbitrary", "arbitrary"),
        name="ffn",
    )(x, g_pre, g_post, wg, wu, wd, *srcs)
    return outs[0], [o.reshape(w.shape[1:]) for o, w in zip(outs[1:], cast_srcs)]


def _rope(p, cos, sin_lo, sin_hi):
    return p * cos + pltpu.roll(p, 8, 1) * sin_hi + pltpu.roll(p, LANES - 8, 1) * sin_lo


def _dup_halves(p):
    lane = lax.broadcasted_iota(jnp.int32, p.shape, 1)
    lo = jnp.where(lane < HEAD_DIM, p, 0.0)
    hi = p - lo
    return lo + pltpu.roll(lo, HEAD_DIM, 1), hi + pltpu.roll(hi, HEAD_DIM, 1)


def _mixer_in_rows(rows, x_ref, g_ref, w_ref, cos_ref, slo_ref, shi_ref, q_ref, kv_ref, u_ref, v_ref):
    h = _rms(x_ref[rows, :], g_ref[...]).astype(BF16)
    cos, slo, shi = cos_ref[rows, :], slo_ref[rows, :], shi_ref[rows, :]

    def proj(c0, width):
        return jnp.dot(h, w_ref[:, c0:c0 + width], preferred_element_type=F32)

    chunk = 2 * LANES
    for c in range(ATTN_WIDTH // chunk):
        z = proj(c * chunk, chunk)
        for n in range(2):
            q_ref[rows, c * chunk + n * LANES:c * chunk + (n + 1) * LANES] = (_rope(
                z[:, n * LANES:(n + 1) * LANES], cos, slo, shi) * (HEAD_DIM ** -0.5)).astype(BF16)
    z = proj(ATTN_WIDTH, chunk)
    k = _rope(z[:, :LANES], cos, slo, shi)
    v = z[:, LANES:]
    for n, part in enumerate(_dup_halves(k) + _dup_halves(v)):
        kv_ref[rows, n * LANES:(n + 1) * LANES] = part.astype(BF16)
    c0 = ATTN_WIDTH + 2 * LANES
    for c in range(SGU_WIDTH // chunk):
        u_ref[rows, c * chunk:(c + 1) * chunk] = jax.nn.gelu(proj(c0 + c * chunk, chunk)).astype(BF16)
        v_ref[rows, c * chunk:(c + 1) * chunk] = jax.nn.gelu(proj(c0 + SGU_WIDTH + c * chunk, chunk)).astype(BF16)


def _mixer_in_body(x_ref, *refs):
    for r0 in range(0, x_ref.shape[0], ROW_SUB):
        _mixer_in_rows(slice(r0, r0 + ROW_SUB), x_ref, *refs)


def _mixer_in(x, g, w_in, cos, slo, shi, *, tm=512):
    t, d = x.shape
    n_in = w_in.shape[1]
    seq_tiles = SEQ // tm
    row = lambda i: (i, 0)
    pos = lambda i: (i % seq_tiles, 0)
    const = lambda i: (0, 0)
    return pl.pallas_call(
        _mixer_in_body,
        grid=(t // tm,),
        in_specs=[
            pl.BlockSpec((tm, d), row),
            pl.BlockSpec((1, d), const),
            pl.BlockSpec((d, n_in), const),
            pl.BlockSpec((tm, LANES), pos),
            pl.BlockSpec((tm, LANES), pos),
            pl.BlockSpec((tm, LANES), pos),
        ],
        out_specs=[
            pl.BlockSpec((tm, ATTN_WIDTH), row),
            pl.BlockSpec((tm, 4 * LANES), row),
            pl.BlockSpec((tm, SGU_WIDTH), row),
            pl.BlockSpec((tm, SGU_WIDTH), row),
        ],
        out_shape=[
            jax.ShapeDtypeStruct((t, ATTN_WIDTH), BF16),
            jax.ShapeDtypeStruct((t, 4 * LANES), BF16),
            jax.ShapeDtypeStruct((t, SGU_WIDTH), BF16),
            jax.ShapeDtypeStruct((t, SGU_WIDTH), BF16),
        ],
        compiler_params=_params("parallel"),
        name="mixer_in",
    )(x, g, w_in, cos, slo, shi)


def _swa_block(q, kv_prev, kv_cur, sinks_ref, valid):
    lane = lax.broadcasted_iota(jnp.int32, (BLOCK, LANES), 1)
    lo_half = lane < HEAD_DIM
    zero = jnp.zeros((), BF16)
    outs = []
    for g in range(N_KV_HEADS):
        kd = jnp.concatenate([kv_prev[:, g * LANES:(g + 1) * LANES], kv_cur[:, g * LANES:(g + 1) * LANES]], axis=0)
        vd = jnp.concatenate([kv_prev[:, (2 + g) * LANES:(3 + g) * LANES],
                              kv_cur[:, (2 + g) * LANES:(3 + g) * LANES]], axis=0)
        parts = []
        for p in range(Q_PER_KV // 2):
            pair = q[:, (g * Q_PER_KV // 2 + p) * LANES:(g * Q_PER_KV // 2 + p + 1) * LANES]
            parts.append(jnp.where(lo_half, pair, zero))
            parts.append(jnp.where(lo_half, zero, pair))
        qs = jnp.concatenate(parts, axis=0)
        s_all = lax.dot_general(qs, kd, (((1,), (1,)), ((), ())), preferred_element_type=F32)
        es, inv = [], []
        for hq in range(Q_PER_KV):
            sink = sinks_ref[g * Q_PER_KV + hq]
            s = jnp.where(valid, s_all[hq * BLOCK:(hq + 1) * BLOCK], -1e30)
            m = jnp.maximum(jnp.max(s, axis=-1, keepdims=True), sink)
            e = jnp.exp(s - m)
            inv.append(1.0 / (jnp.sum(e, axis=-1, keepdims=True) + jnp.exp(sink - m)))
            es.append(e.astype(BF16))
        o_all = jnp.dot(jnp.concatenate(es, axis=0), vd, preferred_element_type=F32)
        for p in range(Q_PER_KV // 2):
            even = o_all[(2 * p) * BLOCK:(2 * p + 1) * BLOCK] * inv[2 * p]
            odd = o_all[(2 * p + 1) * BLOCK:(2 * p + 2) * BLOCK] * inv[2 * p + 1]
            outs.append(jnp.where(lo_half, even, odd).astype(BF16))
    return jnp.concatenate(outs, axis=1)


def _sgu_block(u, v, ln_g, ln_b, ws, bt_ref):
    vf = v.astype(F32)
    mu = jnp.mean(vf, axis=-1, keepdims=True)
    vc = vf - mu
    var = jnp.mean(vc * vc, axis=-1, keepdims=True)
    vn = (vc * lax.rsqrt(var + EPS) * ln_g + ln_b).astype(BF16)
    outs = []
    for g in range(SGU_GROUPS):
        mixed = jnp.dot(ws[g], vn[:, g * LANES:(g + 1) * LANES], preferred_element_type=F32) + bt_ref[:, g:g + 1]
        outs.append((u[:, g * LANES:(g + 1) * LANES].astype(F32) * mixed).astype(BF16))
    return jnp.concatenate(outs, axis=1)


def _mixer_out_body(sinks_ref, x_ref, q_ref, kv_ref, kvp_ref, u_ref, v_ref, lng_ref, lnb_ref, sw_ref, sbt_ref,
                    wout_ref, g_ref, o_ref, mix_ref):
    tm = x_ref.shape[0]
    nblk = tm // BLOCK
    blocks_per_seq = SEQ // BLOCK
    first_blk = pl.program_id(0) * nblk
    ri = lax.broadcasted_iota(jnp.int32, (BLOCK, BLOCK), 0)
    ci = lax.broadcasted_iota(jnp.int32, (BLOCK, BLOCK), 1)
    ws = [jnp.where(ci <= ri, sw_ref[g], 0.0).astype(BF16) for g in range(SGU_GROUPS)]
    qi = lax.broadcasted_iota(jnp.int32, (BLOCK, 2 * BLOCK), 0)
    kj = lax.broadcasted_iota(jnp.int32, (BLOCK, 2 * BLOCK), 1)
    rel = qi + BLOCK - kj
    band = (rel >= 0) & (rel < BLOCK)
    for b in range(nblk):
        r0 = b * BLOCK
        kv_cur = kv_ref[r0:r0 + BLOCK, :]
        kv_prev = kvp_ref[...] if b == 0 else kv_ref[r0 - BLOCK:r0, :]
        first_key = jnp.where(((first_blk + b) % blocks_per_seq) != 0, 0, BLOCK)
        valid = band & (kj >= first_key)
        mix_ref[r0:r0 + BLOCK, :ATTN_WIDTH] = _swa_block(q_ref[r0:r0 + BLOCK, :], kv_prev, kv_cur, sinks_ref, valid)
        mix_ref[r0:r0 + BLOCK, ATTN_WIDTH:] = _sgu_block(u_ref[r0:r0 + BLOCK, :], v_ref[r0:r0 + BLOCK, :],
                                                        lng_ref[...], lnb_ref[...], ws, sbt_ref)
    m = jnp.dot(mix_ref[...], wout_ref[...], preferred_element_type=F32)
    o_ref[...] = x_ref[...] + _rms(m, g_ref[...])


def _mixer_out(x, q, kv, u, v, sinks, ln_g, ln_b, sgu_w, sgu_bt, w_out, g, *, tm=512):
    t, d = x.shape
    nblk = tm // BLOCK
    row = lambda i: (i, 0)
    const = lambda i: (0, 0)
    return pl.pallas_call(
        _mixer_out_body,
        grid=(t // tm,),
        in_specs=[
            pl.BlockSpec(memory_space=pltpu.SMEM),
            pl.BlockSpec((tm, d), row),
            pl.BlockSpec((tm, ATTN_WIDTH), row),
            pl.BlockSpec((tm, 4 * LANES), row),
            pl.BlockSpec((BLOCK, 4 * LANES), lambda i: (jnp.maximum(i * nblk - 1, 0), 0)),
            pl.BlockSpec((tm, SGU_WIDTH), row),
            pl.BlockSpec((tm, SGU_WIDTH), row),
            pl.BlockSpec((1, SGU_WIDTH), const),
            pl.BlockSpec((1, SGU_WIDTH), const),
            pl.BlockSpec((SGU_GROUPS, BLOCK, BLOCK), lambda i: (0, 0, 0)),
            pl.BlockSpec((BLOCK, SGU_GROUPS), const),
            pl.BlockSpec((ATTN_WIDTH + SGU_WIDTH, d), const),
            pl.BlockSpec((1, d), const),
        ],
        out_specs=pl.BlockSpec((tm, d), row),
        out_shape=jax.ShapeDtypeStruct((t, d), F32),
        scratch_shapes=[pltpu.VMEM((tm, ATTN_WIDTH + SGU_WIDTH), BF16)],
        compiler_params=_params("parallel"),
        name="mixer_out",
    )(sinks, x, q, kv, kv, u, v, ln_g, ln_b, sgu_w, sgu_bt, w_out, g)


def _pool_body(x_ref, halo_ref, gpre_ref, w_ref, scale_ref, gpost_ref, o_ref):
    tm, d = x_ref.shape
    gd = d // len(POOL_WINDOWS)
    seq_start = (pl.program_id(0) * tm) % SEQ == 0
    x = x_ref[...]
    h = _rms(x, gpre_ref[...])
    h_halo = jnp.where(seq_start, 0.0, _rms(halo_ref[...], gpre_ref[...]))
    ext = jnp.concatenate([h_halo, h], axis=0)
    t_pos = (pl.program_id(0) * tm) % SEQ + lax.broadcasted_iota(jnp.int32, (tm, 1), 0)
    count = (t_pos + 1).astype(F32)
    ys = []
    for gi, w in enumerate(POOL_WINDOWS):
        s = ext[:, gi * gd:(gi + 1) * gd]
        span = 1
        while span < w:
            s = s + pltpu.roll(s, span, 0)
            span *= 2
        mean = s[POOL_HALO:] * (1.0 / jnp.minimum(count, float(w)))
        pooled = (mean - h[:, gi * gd:(gi + 1) * gd]).astype(BF16)
        ys.append(jnp.dot(pooled, w_ref[gi], preferred_element_type=F32))
    y = jnp.concatenate(ys, axis=1) * scale_ref[...]
    o_ref[...] = x + _rms(y, gpost_ref[...])


def _pool(x, g_pre, pool_w, pool_scale, g_post, *, tm=1024):
    t, d = x.shape
    gd = d // len(POOL_WINDOWS)
    halo_blocks = tm // POOL_HALO
    row = lambda i: (i, 0)
    const = lambda i: (0, 0)
    return pl.pallas_call(
        _pool_body,
        grid=(t // tm,),
        in_specs=[
            pl.BlockSpec((tm, d), row),
            pl.BlockSpec((POOL_HALO, d), lambda i: (jnp.maximum(i * halo_blocks - 1, 0), 0)),
            pl.BlockSpec((1, d), const),
            pl.BlockSpec((len(POOL_WINDOWS), gd, gd), lambda i: (0, 0, 0)),
            pl.BlockSpec((1, d), const),
            pl.BlockSpec((1, d), const),
        ],
        out_specs=pl.BlockSpec((tm, d), row),
        out_shape=jax.ShapeDtypeStruct((t, d), F32),
        compiler_params=_params("parallel"),
        name="pool",
    )(x, x, g_pre, pool_w, pool_scale, g_post)


def _mem_kv_body(m_ref, g_ref, wk_ref, wv_ref, k_ref, v_ref):
    h = _rms(m_ref[...], g_ref[...]).astype(BF16)
    k_ref[...] = jnp.dot(h, wk_ref[...], preferred_element_type=F32).astype(BF16)
    v_ref[...] = jnp.dot(h, wv_ref[...], preferred_element_type=F32).astype(BF16)


def _mem_kv(mem, g, wk, wv, *, tm=256):
    t, d = mem.shape
    n = wk.shape[1]
    row = lambda i: (i, 0)
    const = lambda i: (0, 0)
    return pl.pallas_call(
        _mem_kv_body,
        grid=(t // tm,),
        in_specs=[pl.BlockSpec((tm, d), row), pl.BlockSpec((1, d), const),
                  pl.BlockSpec((d, n), const), pl.BlockSpec((d, n), const)],
        out_specs=[pl.BlockSpec((tm, n), row), pl.BlockSpec((tm, n), row)],
        out_shape=[jax.ShapeDtypeStruct((t, n), BF16), jax.ShapeDtypeStruct((t, n), BF16)],
        compiler_params=_params("parallel"),
        name="mem_kv",
    )(mem, g, wk, wv)


def _cross_body(x_ref, gpre_ref, wq_ref, k_ref, v_ref, wo_ref, gpost_ref, o_ref):
    x = x_ref[...]
    h = _rms(x, gpre_ref[...]).astype(BF16)
    q = jnp.dot(h, wq_ref[...], preferred_element_type=F32).astype(BF16)
    outs = []
    for hh in range(X_HEADS):
        sl = slice(hh * X_HEAD_DIM, (hh + 1) * X_HEAD_DIM)
        s = lax.dot_general(q[:, sl], k_ref[:, sl], (((1,), (1,)), ((), ())),
                            preferred_element_type=F32) * (X_HEAD_DIM ** -0.5)
        e = jnp.exp(s - jnp.max(s, axis=-1, keepdims=True))
        denom = jnp.sum(e, axis=-1, keepdims=True)
        o = jnp.dot(e.astype(BF16), v_ref[:, sl], preferred_element_type=F32) * (1.0 / denom)
        outs.append(o.astype(BF16))
    y = jnp.dot(jnp.concatenate(outs, axis=1), wo_ref[...], preferred_element_type=F32)
    o_ref[...] = x + _rms(y, gpost_ref[...])


def _cross(x, g_pre, wq, k, v, wo, g_post, *, tm=1024):
    t, d = x.shape
    n = wq.shape[1]
    mem_len = k.shape[0] // (t // SEQ)
    tiles_per_seq = SEQ // tm
    row = lambda i: (i, 0)
    const = lambda i: (0, 0)
    batch = lambda i: (i // tiles_per_seq, 0)
    return pl.pallas_call(
        _cross_body,
        grid=(t // tm,),
        in_specs=[
            pl.BlockSpec((tm, d), row),
            pl.BlockSpec((1, d), const),
            pl.BlockSpec((d, n), const),
            pl.BlockSpec((mem_len, n), batch),
            pl.BlockSpec((mem_len, n), batch),
            pl.BlockSpec((n, d), const),
            pl.BlockSpec((1, d), const),
        ],
        out_specs=pl.BlockSpec((tm, d), row),
        out_shape=jax.ShapeDtypeStruct((t, d), F32),
        compiler_params=_params("parallel"),
        name="cross",
    )(x, g_pre, wq, k, v, wo, g_post)


def _rope_tables():
    half = ROPE_DIM // 2
    inv = ROPE_THETA ** (-jnp.arange(half, dtype=F32) * 2.0 / ROPE_DIM)
    ang = jnp.arange(SEQ, dtype=F32)[:, None] * inv[None, :]
    cos, sin = jnp.cos(ang), jnp.sin(ang)
    ones = jnp.ones((SEQ, HEAD_DIM - ROPE_DIM), F32)
    zeros = jnp.zeros((SEQ, HEAD_DIM - ROPE_DIM), F32)
    z8 = jnp.zeros((SEQ, half), F32)
    cos_h = jnp.concatenate([cos, cos, ones], axis=1)
    slo_h = jnp.concatenate([-sin, z8, zeros], axis=1)
    shi_h = jnp.concatenate([z8, sin, zeros], axis=1)
    two = lambda a: jnp.concatenate([a, a], axis=1)
    return two(cos_h), two(slo_h), two(shi_h)


def kernel(x, mem, norms, mem_norm, ffn1_wg, ffn1_wu, ffn1_wd, ffn2_wg, ffn2_wu, ffn2_wd, x_wq, x_wk, x_wv, x_wo,
           mix_w_in, mix_w_out, attn_sinks, sgu_ln_g, sgu_ln_b, sgu_w, sgu_b, pool_w, pool_scale):
    b, s, d = x.shape
    depth = norms.shape[0]
    assert s == SEQ
    bf = lambda a: a.astype(BF16)
    xt = x.reshape(b * s, d)
    memt = mem.reshape(b * mem.shape[1], d)
    cos, slo, shi = _rope_tables()
    ffn1 = (ffn1_wg, ffn1_wu, ffn1_wd)
    ffn2 = (ffn2_wg, ffn2_wu, ffn2_wd)
    w_next = [bf(w[0]) for w in ffn1]
    for layer in range(depth):
        g = norms[layer][:, None, :]
        xt, w_next = _ffn(xt, g[0], g[1], *w_next, ffn2, layer)
        i = layer // 2
        if layer % 2 == 0:
            q, kv, u, v = _mixer_in(xt, g[2], bf(mix_w_in[i]), cos, slo, shi)
            xt = _mixer_out(xt, q, kv, u, v, attn_sinks[i], sgu_ln_g[i][None], sgu_ln_b[i][None], sgu_w[i],
                            sgu_b[i].T, bf(mix_w_out[i]), g[3])
        else:
            xt = _pool(xt, g[2], bf(pool_w[i]), pool_scale[i][None], g[3])
        kx, vx = _mem_kv(memt, mem_norm[layer][None], bf(x_wk[layer]), bf(x_wv[layer]))
        xt = _cross(xt, g[4], bf(x_wq[layer]), kx, vx, bf(x_wo[layer]), g[5])
        more = layer + 1 < depth
        xt, w_next = _ffn(xt, g[6], g[7], *w_next, ffn1 if more else (), layer + 1 if more else 0)
    return xt.reshape(b, s, d)
```

```python
import functools

import jax
import jax.numpy as jnp
from jax import lax
from jax.experimental import pallas as pl
from jax.experimental.pallas import tpu as pltpu

F32 = jnp.float32
BF16 = jnp.bfloat16

EPS = 1e-6
SEQ = 4096
HEAD_DIM = 64
N_Q_HEADS = 16
N_KV_HEADS = 2
Q_PER_KV = N_Q_HEADS // N_KV_HEADS
ATTN_WIDTH = N_Q_HEADS * HEAD_DIM
BLOCK = 128
ROPE_DIM = HEAD_DIM // 4
ROPE_THETA = 500000.0
SGU_GROUPS = 8
SGU_WIDTH = 1024
POOL_WINDOWS = (2, 4, 8, 16)
POOL_HALO = 16
X_HEADS = 4
X_HEAD_DIM = 128
LANES = 128
ROW_SUB = 256
FFN_SUB = 256
FFN_OUT = 512
BF16_SUBLANES = 16
FFN_ROWS = 16

VMEM_LIMIT_BYTES = 56 * 1024 * 1024


def _rms(x, g):
    return x * lax.rsqrt(jnp.mean(x * x, axis=-1, keepdims=True) + EPS) * g


def _params(*sem, vmem=VMEM_LIMIT_BYTES):
    return pltpu.CompilerParams(dimension_semantics=sem, vmem_limit_bytes=vmem)


def _ffn_body(n_cast, x_ref, gpre_ref, gpost_ref, wg_ref, wu_ref, wd_ref, *refs):
    src_refs, o_ref, dst_refs, h_ref = refs[:n_cast], refs[n_cast], refs[n_cast + 1:-1], refs[-1]
    j = pl.program_id(1)
    last = pl.num_programs(1) - 1
    tm, d = x_ref.shape
    fk = wg_ref.shape[1]

    def cast_next_weights():
        for src_ref, dst_ref in zip(src_refs, dst_refs):
            dst_ref[...] = src_ref[...].astype(BF16)

    def prologue(r0):
        for r in range(r0, r0 + ROW_SUB, FFN_ROWS):
            rows = slice(r, r + FFN_ROWS)
            h_ref[rows, :] = _rms(x_ref[rows, :], gpre_ref[...]).astype(BF16)
            o_ref[rows, :] = jnp.zeros((FFN_ROWS, d), F32)

    def epilogue(r0):
        for r in range(r0, r0 + ROW_SUB, FFN_ROWS):
            rows = slice(r, r + FFN_ROWS)
            o_ref[rows, :] = x_ref[rows, :] + 0.5 * _rms(o_ref[rows, :], gpost_ref[...])

    def chunk(rows, cols):
        h = h_ref[rows, :]
        gate = jnp.dot(h, wg_ref[:, cols], preferred_element_type=F32)
        up = jnp.dot(h, wu_ref[:, cols], preferred_element_type=F32)
        a = (jax.nn.silu(gate) * up).astype(BF16)
        for n0 in range(0, d, FFN_OUT):
            o_ref[rows, n0:n0 + FFN_OUT] += jnp.dot(a, wd_ref[cols, n0:n0 + FFN_OUT], preferred_element_type=F32)

    @pl.when(j == 0)
    def _():
        cast_next_weights()
        for r0 in range(0, tm, ROW_SUB):
            prologue(r0)
            chunk(slice(r0, r0 + ROW_SUB), slice(0, fk))

    @pl.when((j > 0) & (j < last))
    def _():
        cast_next_weights()
        for c0 in range(0, fk, FFN_SUB):
            chunk(slice(0, tm), slice(c0, c0 + FFN_SUB))

    @pl.when(j == last)
    def _():
        cast_next_weights()
        for r0 in range(0, tm, ROW_SUB):
            chunk(slice(r0, r0 + ROW_SUB), slice(0, fk))
            epilogue(r0)


def _ffn(x, g_pre, g_post, wg, wu, wd, cast_srcs=(), cast_layer=0, *, tm=1024, fk=512):
    t, d = x.shape
    f = wg.shape[1]
    ni, nj = t // tm, f // fk
    cast_specs, dst_specs, dst_shapes = [], [], []
    for w in cast_srcs:
        if w.shape[1] == d:
            blk, idx = (d // ni, f // nj), (lambda i, j: (i, j))
        else:
            blk, idx = (f // nj, d // ni), (lambda i, j: (j, i))
        assert blk[0] % BF16_SUBLANES == 0 and blk[1] % LANES == 0
        cast_specs.append(pl.BlockSpec((None,) + blk, lambda i, j, idx=idx: (cast_layer,) + idx(i, j)))
        dst_specs.append(pl.BlockSpec(blk, idx))
        dst_shapes.append(jax.ShapeDtypeStruct(w.shape[1:], BF16))
    outs = pl.pallas_call(
        functools.partial(_ffn_body, len(cast_srcs)),
        grid=(ni, nj),
        in_specs=[
            pl.BlockSpec((tm, d), lambda i, j: (i, 0)),
            pl.BlockSpec((1, d), lambda i, j: (0, 0)),
            pl.BlockSpec((1, d), lambda i, j: (0, 0)),
            pl.BlockSpec((d, fk), lambda i, j: (0, j)),
            pl.BlockSpec((d, fk), lambda i, j: (0, j)),
            pl.BlockSpec((fk, d), lambda i, j: (j, 0)),
        ] + cast_specs,
        out_specs=[pl.BlockSpec((tm, d), lambda i, j: (i, 0))] + dst_specs,
        out_shape=[jax.ShapeDtypeStruct((t, d), F32)] + dst_shapes,
        scratch_shapes=[pltpu.VMEM((tm, d), BF16)],
        compiler_params=_params("arbitrary", "arbitrary"),
        name="ffn",
    )(x, g_pre, g_post, wg, wu, wd, *cast_srcs)
    return outs[0], list(outs[1:])


def _rope(p, cos, sin_lo, sin_hi):
    return p * cos + pltpu.roll(p, 8, 1) * sin_hi + pltpu.roll(p, LANES - 8, 1) * sin_lo


def _dup_halves(p):
    lane = lax.broadcasted_iota(jnp.int32, p.shape, 1)
    lo = jnp.where(lane < HEAD_DIM, p, 0.0)
    hi = p - lo
    return lo + pltpu.roll(lo, HEAD_DIM, 1), hi + pltpu.roll(hi, HEAD_DIM, 1)


def _mixer_in_rows(rows, x_ref, g_ref, w_ref, cos_ref, slo_ref, shi_ref, q_ref, kv_ref, u_ref, v_ref):
    h = _rms(x_ref[rows, :], g_ref[...]).astype(BF16)
    cos, slo, shi = cos_ref[rows, :], slo_ref[rows, :], shi_ref[rows, :]

    def proj(c0, width):
        return jnp.dot(h, w_ref[:, c0:c0 + width], preferred_element_type=F32)

    chunk = 2 * LANES
    for c in range(ATTN_WIDTH // chunk):
        z = proj(c * chunk, chunk)
        for n in range(2):
            q_ref[rows, c * chunk + n * LANES:c * chunk + (n + 1) * LANES] = (_rope(
                z[:, n * LANES:(n + 1) * LANES], cos, slo, shi) * (HEAD_DIM ** -0.5)).astype(BF16)
    z = proj(ATTN_WIDTH, chunk)
    k = _rope(z[:, :LANES], cos, slo, shi)
    v = z[:, LANES:]
    for n, part in enumerate(_dup_halves(k) + _dup_halves(v)):
        kv_ref[rows, n * LANES:(n + 1) * LANES] = part.astype(BF16)
    c0 = ATTN_WIDTH + 2 * LANES
    for c in range(SGU_WIDTH // chunk):
        u_ref[rows, c * chunk:(c + 1) * chunk] = jax.nn.gelu(proj(c0 + c * chunk, chunk)).astype(BF16)
        v_ref[rows, c * chunk:(c + 1) * chunk] = jax.nn.gelu(proj(c0 + SGU_WIDTH + c * chunk, chunk)).astype(BF16)


def _mixer_in_body(x_ref, *refs):
    for r0 in range(0, x_ref.shape[0], ROW_SUB):
        _mixer_in_rows(slice(r0, r0 + ROW_SUB), x_ref, *refs)


def _mixer_in(x, g, w_in, cos, slo, shi, *, tm=512):
    t, d = x.shape
    n_in = w_in.shape[1]
    seq_tiles = SEQ // tm
    row = lambda i: (i, 0)
    pos = lambda i: (i % seq_tiles, 0)
    const = lambda i: (0, 0)
    return pl.pallas_call(
        _mixer_in_body,
        grid=(t // tm,),
        in_specs=[
            pl.BlockSpec((tm, d), row),
            pl.BlockSpec((1, d), const),
            pl.BlockSpec((d, n_in), const),
            pl.BlockSpec((tm, LANES), pos),
            pl.BlockSpec((tm, LANES), pos),
            pl.BlockSpec((tm, LANES), pos),
        ],
        out_specs=[
            pl.BlockSpec((tm, ATTN_WIDTH), row),
            pl.BlockSpec((tm, 4 * LANES), row),
            pl.BlockSpec((tm, SGU_WIDTH), row),
            pl.BlockSpec((tm, SGU_WIDTH), row),
        ],
        out_shape=[
            jax.ShapeDtypeStruct((t, ATTN_WIDTH), BF16),
            jax.ShapeDtypeStruct((t, 4 * LANES), BF16),
            jax.ShapeDtypeStruct((t, SGU_WIDTH), BF16),
            jax.ShapeDtypeStruct((t, SGU_WIDTH), BF16),
        ],
        compiler_params=_params("parallel"),
        name="mixer_in",
    )(x, g, w_in, cos, slo, shi)


def _swa_block(q, kv_prev, kv_cur, sinks_ref, valid):
    lane = lax.broadcasted_iota(jnp.int32, (BLOCK, LANES), 1)
    lo_half = lane < HEAD_DIM
    zero = jnp.zeros((), BF16)
    outs = []
    for g in range(N_KV_HEADS):
        kd = jnp.concatenate([kv_prev[:, g * LANES:(g + 1) * LANES], kv_cur[:, g * LANES:(g + 1) * LANES]], axis=0)
        vd = jnp.concatenate([kv_prev[:, (2 + g) * LANES:(3 + g) * LANES],
                              kv_cur[:, (2 + g) * LANES:(3 + g) * LANES]], axis=0)
        parts = []
        for p in range(Q_PER_KV // 2):
            pair = q[:, (g * Q_PER_KV // 2 + p) * LANES:(g * Q_PER_KV // 2 + p + 1) * LANES]
            parts.append(jnp.where(lo_half, pair, zero))
            parts.append(jnp.where(lo_half, zero, pair))
        qs = jnp.concatenate(parts, axis=0)
        s_all = lax.dot_general(qs, kd, (((1,), (1,)), ((), ())), preferred_element_type=F32)
        es, inv = [], []
        for hq in range(Q_PER_KV):
            sink = sinks_ref[g * Q_PER_KV + hq]
            s = jnp.where(valid, s_all[hq * BLOCK:(hq + 1) * BLOCK], -1e30)
            m = jnp.maximum(jnp.max(s, axis=-1, keepdims=True), sink)
            e = jnp.exp(s - m)
            inv.append(1.0 / (jnp.sum(e, axis=-1, keepdims=True) + jnp.exp(sink - m)))
            es.append(e.astype(BF16))
        o_all = jnp.dot(jnp.concatenate(es, axis=0), vd, preferred_element_type=F32)
        for p in range(Q_PER_KV // 2):
            even = o_all[(2 * p) * BLOCK:(2 * p + 1) * BLOCK] * inv[2 * p]
            odd = o_all[(2 * p + 1) * BLOCK:(2 * p + 2) * BLOCK] * inv[2 * p + 1]
            outs.append(jnp.where(lo_half, even, odd).astype(BF16))
    return jnp.concatenate(outs, axis=1)


def _sgu_block(u, v, ln_g, ln_b, ws, bt_ref):
    vf = v.astype(F32)
    mu = jnp.mean(vf, axis=-1, keepdims=True)
    vc = vf - mu
    var = jnp.mean(vc * vc, axis=-1, keepdims=True)
    vn = (vc * lax.rsqrt(var + EPS) * ln_g + ln_b).astype(BF16)
    outs = []
    for g in range(SGU_GROUPS):
        mixed = jnp.dot(ws[g], vn[:, g * LANES:(g + 1) * LANES], preferred_element_type=F32) + bt_ref[:, g:g + 1]
        outs.append((u[:, g * LANES:(g + 1) * LANES].astype(F32) * mixed).astype(BF16))
    return jnp.concatenate(outs, axis=1)


def _mixer_out_body(sinks_ref, x_ref, q_ref, kv_ref, kvp_ref, u_ref, v_ref, lng_ref, lnb_ref, sw_ref, sbt_ref,
                    wout_ref, g_ref, o_ref, mix_ref):
    tm = x_ref.shape[0]
    nblk = tm // BLOCK
    blocks_per_seq = SEQ // BLOCK
    first_blk = pl.program_id(0) * nblk
    ri = lax.broadcasted_iota(jnp.int32, (BLOCK, BLOCK), 0)
    ci = lax.broadcasted_iota(jnp.int32, (BLOCK, BLOCK), 1)
    ws = [jnp.where(ci <= ri, sw_ref[g], 0.0).astype(BF16) for g in range(SGU_GROUPS)]
    qi = lax.broadcasted_iota(jnp.int32, (BLOCK, 2 * BLOCK), 0)
    kj = lax.broadcasted_iota(jnp.int32, (BLOCK, 2 * BLOCK), 1)
    rel = qi + BLOCK - kj
    band = (rel >= 0) & (rel < BLOCK)
    for b in range(nblk):
        r0 = b * BLOCK
        kv_cur = kv_ref[r0:r0 + BLOCK, :]
        kv_prev = kvp_ref[...] if b == 0 else kv_ref[r0 - BLOCK:r0, :]
        first_key = jnp.where(((first_blk + b) % blocks_per_seq) != 0, 0, BLOCK)
        valid = band & (kj >= first_key)
        mix_ref[r0:r0 + BLOCK, :ATTN_WIDTH] = _swa_block(q_ref[r0:r0 + BLOCK, :], kv_prev, kv_cur, sinks_ref, valid)
        mix_ref[r0:r0 + BLOCK, ATTN_WIDTH:] = _sgu_block(u_ref[r0:r0 + BLOCK, :], v_ref[r0:r0 + BLOCK, :],
                                                        lng_ref[...], lnb_ref[...], ws, sbt_ref)
    m = jnp.dot(mix_ref[...], wout_ref[...], preferred_element_type=F32)
    o_ref[...] = x_ref[...] + _rms(m, g_ref[...])


def _mixer_out(x, q, kv, u, v, sinks, ln_g, ln_b, sgu_w, sgu_bt, w_out, g, *, tm=512):
    t, d = x.shape
    nblk = tm // BLOCK
    row = lambda i: (i, 0)
    const = lambda i: (0, 0)
    return pl.pallas_call(
        _mixer_out_body,
        grid=(t // tm,),
        in_specs=[
            pl.BlockSpec(memory_space=pltpu.SMEM),
            pl.BlockSpec((tm, d), row),
            pl.BlockSpec((tm, ATTN_WIDTH), row),
            pl.BlockSpec((tm, 4 * LANES), row),
            pl.BlockSpec((BLOCK, 4 * LANES), lambda i: (jnp.maximum(i * nblk - 1, 0), 0)),
            pl.BlockSpec((tm, SGU_WIDTH), row),
            pl.BlockSpec((tm, SGU_WIDTH), row),
            pl.BlockSpec((1, SGU_WIDTH), const),
            pl.BlockSpec((1, SGU_WIDTH), const),
            pl.BlockSpec((SGU_GROUPS, BLOCK, BLOCK), lambda i: (0, 0, 0)),
            pl.BlockSpec((BLOCK, SGU_GROUPS), const),
            pl.BlockSpec((ATTN_WIDTH + SGU_WIDTH, d), const),
            pl.BlockSpec((1, d), const),
        ],
        out_specs=pl.BlockSpec((tm, d), row),
        out_shape=jax.ShapeDtypeStruct((t, d), F32),
        scratch_shapes=[pltpu.VMEM((tm, ATTN_WIDTH + SGU_WIDTH), BF16)],
        compiler_params=_params("parallel"),
        name="mixer_out",
    )(sinks, x, q, kv, kv, u, v, ln_g, ln_b, sgu_w, sgu_bt, w_out, g)


def _pool_body(x_ref, halo_ref, gpre_ref, w_ref, scale_ref, gpost_ref, o_ref):
    tm, d = x_ref.shape
    gd = d // len(POOL_WINDOWS)
    seq_start = (pl.program_id(0) * tm) % SEQ == 0
    x = x_ref[...]
    h = _rms(x, gpre_ref[...])
    h_halo = jnp.where(seq_start, 0.0, _rms(halo_ref[...], gpre_ref[...]))
    ext = jnp.concatenate([h_halo, h], axis=0)
    t_pos = (pl.program_id(0) * tm) % SEQ + lax.broadcasted_iota(jnp.int32, (tm, 1), 0)
    count = (t_pos + 1).astype(F32)
    ys = []
    for gi, w in enumerate(POOL_WINDOWS):
        s = ext[:, gi * gd:(gi + 1) * gd]
        span = 1
        while span < w:
            s = s + pltpu.roll(s, span, 0)
            span *= 2
        mean = s[POOL_HALO:] * (1.0 / jnp.minimum(count, float(w)))
        pooled = (mean - h[:, gi * gd:(gi + 1) * gd]).astype(BF16)
        ys.append(jnp.dot(pooled, w_ref[gi], preferred_element_type=F32))
    y = jnp.concatenate(ys, axis=1) * scale_ref[...]
    o_ref[...] = x + _rms(y, gpost_ref[...])


def _pool(x, g_pre, pool_w, pool_scale, g_post, *, tm=1024):
    t, d = x.shape
    gd = d // len(POOL_WINDOWS)
    halo_blocks = tm // POOL_HALO
    row = lambda i: (i, 0)
    const = lambda i: (0, 0)
    return pl.pallas_call(
        _pool_body,
        grid=(t // tm,),
        in_specs=[
            pl.BlockSpec((tm, d), row),
            pl.BlockSpec((POOL_HALO, d), lambda i: (jnp.maximum(i * halo_blocks - 1, 0), 0)),
            pl.BlockSpec((1, d), const),
            pl.BlockSpec((len(POOL_WINDOWS), gd, gd), lambda i: (0, 0, 0)),
            pl.BlockSpec((1, d), const),
            pl.BlockSpec((1, d), const),
        ],
        out_specs=pl.BlockSpec((tm, d), row),
        out_shape=jax.ShapeDtypeStruct((t, d), F32),
        compiler_params=_params("parallel"),
        name="pool",
    )(x, x, g_pre, pool_w, pool_scale, g_post)


def _mem_kv_body(m_ref, g_ref, wk_ref, wv_ref, k_ref, v_ref):
    h = _rms(m_ref[...], g_ref[...]).astype(BF16)
    k_ref[...] = jnp.dot(h, wk_ref[...], preferred_element_type=F32).astype(BF16)
    v_ref[...] = jnp.dot(h, wv_ref[...], preferred_element_type=F32).astype(BF16)


def _mem_kv(mem, g, wk, wv, *, tm=256):
    t, d = mem.shape
    n = wk.shape[1]
    row = lambda i: (i, 0)
    const = lambda i: (0, 0)
    return pl.pallas_call(
        _mem_kv_body,
        grid=(t // tm,),
        in_specs=[pl.BlockSpec((tm, d), row), pl.BlockSpec((1, d), const),
                  pl.BlockSpec((d, n), const), pl.BlockSpec((d, n), const)],
        out_specs=[pl.BlockSpec((tm, n), row), pl.BlockSpec((tm, n), row)],
        out_shape=[jax.ShapeDtypeStruct((t, n), BF16), jax.ShapeDtypeStruct((t, n), BF16)],
        compiler_params=_params("parallel"),
        name="mem_kv",
    )(mem, g, wk, wv)


def _cross_body(x_ref, gpre_ref, wq_ref, k_ref, v_ref, wo_ref, gpost_ref, o_ref):
    x = x_ref[...]
    h = _rms(x, gpre_ref[...]).astype(BF16)
    q = jnp.dot(h, wq_ref[...], preferred_element_type=F32).astype(BF16)
    outs = []
    for hh in range(X_HEADS):
        sl = slice(hh * X_HEAD_DIM, (hh + 1) * X_HEAD_DIM)
        s = lax.dot_general(q[:, sl], k_ref[:, sl], (((1,), (1,)), ((), ())),
                            preferred_element_type=F32) * (X_HEAD_DIM ** -0.5)
        e = jnp.exp(s - jnp.max(s, axis=-1, keepdims=True))
        denom = jnp.sum(e, axis=-1, keepdims=True)
        o = jnp.dot(e.astype(BF16), v_ref[:, sl], preferred_element_type=F32) * (1.0 / denom)
        outs.append(o.astype(BF16))
    y = jnp.dot(jnp.concatenate(outs, axis=1), wo_ref[...], preferred_element_type=F32)
    o_ref[...] = x + _rms(y, gpost_ref[...])


def _cross(x, g_pre, wq, k, v, wo, g_post, *, tm=1024):
    t, d = x.shape
    n = wq.shape[1]
    mem_len = k.shape[0] // (t // SEQ)
    tiles_per_seq = SEQ // tm
    row = lambda i: (i, 0)
    const = lambda i: (0, 0)
    batch = lambda i: (i // tiles_per_seq, 0)
    return pl.pallas_call(
        _cross_body,
        grid=(t // tm,),
        in_specs=[
            pl.BlockSpec((tm, d), row),
            pl.BlockSpec((1, d), const),
            pl.BlockSpec((d, n), const),
            pl.BlockSpec((mem_len, n), batch),
            pl.BlockSpec((mem_len, n), batch),
            pl.BlockSpec((n, d), const),
            pl.BlockSpec((1, d), const),
        ],
        out_specs=pl.BlockSpec((tm, d), row),
        out_shape=jax.ShapeDtypeStruct((t, d), F32),
        compiler_params=_params("parallel"),
        name="cross",
    )(x, g_pre, wq, k, v, wo, g_post)


def _rope_tables():
    half = ROPE_DIM // 2
    inv = ROPE_THETA ** (-jnp.arange(half, dtype=F32) * 2.0 / ROPE_DIM)
    ang = jnp.arange(SEQ, dtype=F32)[:, None] * inv[None, :]
    cos, sin = jnp.cos(ang), jnp.sin(ang)
    ones = jnp.ones((SEQ, HEAD_DIM - ROPE_DIM), F32)
    zeros = jnp.zeros((SEQ, HEAD_DIM - ROPE_DIM), F32)
    z8 = jnp.zeros((SEQ, half), F32)
    cos_h = jnp.concatenate([cos, cos, ones], axis=1)
    slo_h = jnp.concatenate([-sin, z8, zeros], axis=1)
    shi_h = jnp.concatenate([z8, sin, zeros], axis=1)
    two = lambda a: jnp.concatenate([a, a], axis=1)
    return two(cos_h), two(slo_h), two(shi_h)


def kernel(x, mem, norms, mem_norm, ffn1_wg, ffn1_wu, ffn1_wd, ffn2_wg, ffn2_wu, ffn2_wd, x_wq, x_wk, x_wv, x_wo,
           mix_w_in, mix_w_out, attn_sinks, sgu_ln_g, sgu_ln_b, sgu_w, sgu_b, pool_w, pool_scale):
    b, s, d = x.shape
    depth = norms.shape[0]
    assert s == SEQ
    bf = lambda a: a.astype(BF16)
    xt = x.reshape(b * s, d)
    memt = mem.reshape(b * mem.shape[1], d)
    cos, slo, shi = _rope_tables()
    ffn1 = (ffn1_wg, ffn1_wu, ffn1_wd)
    ffn2 = (ffn2_wg, ffn2_wu, ffn2_wd)
    w_next = [bf(w[0]) for w in ffn1]
    for layer in range(depth):
        g = norms[layer][:, None, :]
        xt, w_next = _ffn(xt, g[0], g[1], *w_next, ffn2, layer)
        i = layer // 2
        if layer % 2 == 0:
            q, kv, u, v = _mixer_in(xt, g[2], bf(mix_w_in[i]), cos, slo, shi)
            xt = _mixer_out(xt, q, kv, u, v, attn_sinks[i], sgu_ln_g[i][None], sgu_ln_b[i][None], sgu_w[i],
                            sgu_b[i].T, bf(mix_w_out[i]), g[3])
        else:
            xt = _pool(xt, g[2], bf(pool_w[i]), pool_scale[i][None], g[3])
        kx, vx = _mem_kv(memt, mem_norm[layer][None], bf(x_wk[layer]), bf(x_wv[layer]))
        xt = _cross(xt, g[4], bf(x_wq[layer]), kx, vx, bf(x_wo[layer]), g[5])
        more = layer + 1 < depth
        xt, w_next = _ffn(xt, g[6], g[7], *w_next, ffn1 if more else (), layer + 1 if more else 0)
    return xt.reshape(b, s, d)
```

```python
import functools

import jax
import jax.numpy as jnp
from jax import lax
from jax.experimental import pallas as pl
from jax.experimental.pallas import tpu as pltpu

F32 = jnp.float32
BF16 = jnp.bfloat16

EPS = 1e-6
SEQ = 4096
HEAD_DIM = 64
N_Q_HEADS = 16
N_KV_HEADS = 2
Q_PER_KV = N_Q_HEADS // N_KV_HEADS
ATTN_WIDTH = N_Q_HEADS * HEAD_DIM
BLOCK = 128
ROPE_DIM = HEAD_DIM // 4
ROPE_THETA = 500000.0
SGU_GROUPS = 8
SGU_WIDTH = 1024
POOL_WINDOWS = (2, 4, 8, 16)
POOL_HALO = 16
X_HEADS = 4
X_HEAD_DIM = 128
LANES = 128
ROW_SUB = 256
FFN_SUB = 256
FFN_OUT = 512
BF16_SUBLANES = 16
FFN_ROWS = 16

VMEM_LIMIT_BYTES = 56 * 1024 * 1024


def _rms(x, g):
    return x * lax.rsqrt(jnp.mean(x * x, axis=-1, keepdims=True) + EPS) * g


def _params(*sem, vmem=VMEM_LIMIT_BYTES):
    return pltpu.CompilerParams(dimension_semantics=sem, vmem_limit_bytes=vmem)


def _ffn_body(n_cast, x_ref, gpre_ref, gpost_ref, wg_ref, wu_ref, wd_ref, *refs):
    src_refs, o_ref, dst_refs, h_ref = refs[:n_cast], refs[n_cast], refs[n_cast + 1:-1], refs[-1]
    j = pl.program_id(1)
    last = pl.num_programs(1) - 1
    tm, d = x_ref.shape
    fk = wg_ref.shape[1]

    def cast_next_weights():
        for src_ref, dst_ref in zip(src_refs, dst_refs):
            dst_ref[...] = src_ref[...].astype(BF16)

    def prologue(r0):
        for r in range(r0, r0 + ROW_SUB, FFN_ROWS):
            rows = slice(r, r + FFN_ROWS)
            h_ref[rows, :] = _rms(x_ref[rows, :], gpre_ref[...]).astype(BF16)
            o_ref[rows, :] = jnp.zeros((FFN_ROWS, d), F32)

    def epilogue(r0):
        for r in range(r0, r0 + ROW_SUB, FFN_ROWS):
            rows = slice(r, r + FFN_ROWS)
            o_ref[rows, :] = x_ref[rows, :] + 0.5 * _rms(o_ref[rows, :], gpost_ref[...])

    def chunk(rows, cols):
        h = h_ref[rows, :]
        gate = jnp.dot(h, wg_ref[:, cols], preferred_element_type=F32)
        up = jnp.dot(h, wu_ref[:, cols], preferred_element_type=F32)
        a = (jax.nn.silu(gate) * up).astype(BF16)
        for n0 in range(0, d, FFN_OUT):
            o_ref[rows, n0:n0 + FFN_OUT] += jnp.dot(a, wd_ref[cols, n0:n0 + FFN_OUT], preferred_element_type=F32)

    @pl.when(j == 0)
    def _():
        cast_next_weights()
        for r0 in range(0, tm, ROW_SUB):
            prologue(r0)
            chunk(slice(r0, r0 + ROW_SUB), slice(0, fk))

    @pl.when((j > 0) & (j < last))
    def _():
        cast_next_weights()
        for c0 in range(0, fk, FFN_SUB):
            chunk(slice(0, tm), slice(c0, c0 + FFN_SUB))

    @pl.when(j == last)
    def _():
        cast_next_weights()
        for r0 in range(0, tm, ROW_SUB):
            chunk(slice(r0, r0 + ROW_SUB), slice(0, fk))
            epilogue(r0)


def _ffn(x, g_pre, g_post, wg, wu, wd, cast_srcs=(), cast_layer=0, *, tm=1024, fk=512):
    t, d = x.shape
    f = wg.shape[1]
    ni, nj = t // tm, f // fk
    cast_specs, dst_specs, dst_shapes = [], [], []
    for w in cast_srcs:
        if w.shape[1] == d:
            blk, idx = (d // ni, f // nj), (lambda i, j: (i, j))
        else:
            blk, idx = (f // nj, d // ni), (lambda i, j: (j, i))
        assert blk[0] % BF16_SUBLANES == 0 and blk[1] % LANES == 0
        cast_specs.append(pl.BlockSpec((None,) + blk, lambda i, j, idx=idx: (cast_layer,) + idx(i, j)))
        dst_specs.append(pl.BlockSpec(blk, idx))
        dst_shapes.append(jax.ShapeDtypeStruct(w.shape[1:], BF16))
    outs = pl.pallas_call(
        functools.partial(_ffn_body, len(cast_srcs)),
        grid=(ni, nj),
        in_specs=[
            pl.BlockSpec((tm, d), lambda i, j: (i, 0)),
            pl.BlockSpec((1, d), lambda i, j: (0, 0)),
            pl.BlockSpec((1, d), lambda i, j: (0, 0)),
            pl.BlockSpec((d, fk), lambda i, j: (0, j)),
            pl.BlockSpec((d, fk), lambda i, j: (0, j)),
            pl.BlockSpec((fk, d), lambda i, j: (j, 0)),
        ] + cast_specs,
        out_specs=[pl.BlockSpec((tm, d), lambda i, j: (i, 0))] + dst_specs,
        out_shape=[jax.ShapeDtypeStruct((t, d), F32)] + dst_shapes,
        scratch_shapes=[pltpu.VMEM((tm, d), BF16)],
        compiler_params=_params("arbitrary", "arbitrary"),
        name="ffn",
    )(x, g_pre, g_post, wg, wu, wd, *cast_srcs)
    return outs[0], list(outs[1:])


def _rope(p, cos, sin_lo, sin_hi):
    return p * cos + pltpu.roll(p, 8, 1) * sin_hi + pltpu.roll(p, LANES - 8, 1) * sin_lo


def _dup_halves(p):
    lane = lax.broadcasted_iota(jnp.int32, p.shape, 1)
    lo = jnp.where(lane < HEAD_DIM, p, 0.0)
    hi = p - lo
    return lo + pltpu.roll(lo, HEAD_DIM, 1), hi + pltpu.roll(hi, HEAD_DIM, 1)


def _mixer_in_rows(rows, x_ref, g_ref, w_ref, cos_ref, slo_ref, shi_ref, q_ref, kv_ref, u_ref, v_ref):
    h = _rms(x_ref[rows, :], g_ref[...]).astype(BF16)
    cos, slo, shi = cos_ref[rows, :], slo_ref[rows, :], shi_ref[rows, :]

    def proj(c0, width):
        return jnp.dot(h, w_ref[:, c0:c0 + width], preferred_element_type=F32)

    chunk = 2 * LANES
    for c in range(ATTN_WIDTH // chunk):
        z = proj(c * chunk, chunk)
        for n in range(2):
            q_ref[rows, c * chunk + n * LANES:c * chunk + (n + 1) * LANES] = (_rope(
                z[:, n * LANES:(n + 1) * LANES], cos, slo, shi) * (HEAD_DIM ** -0.5)).astype(BF16)
    z = proj(ATTN_WIDTH, chunk)
    k = _rope(z[:, :LANES], cos, slo, shi)
    v = z[:, LANES:]
    for n, part in enumerate(_dup_halves(k) + _dup_halves(v)):
        kv_ref[rows, n * LANES:(n + 1) * LANES] = part.astype(BF16)
    c0 = ATTN_WIDTH + 2 * LANES
    for c in range(SGU_WIDTH // chunk):
        u_ref[rows, c * chunk:(c + 1) * chunk] = jax.nn.gelu(proj(c0 + c * chunk, chunk)).astype(BF16)
        v_ref[rows, c * chunk:(c + 1) * chunk] = jax.nn.gelu(proj(c0 + SGU_WIDTH + c * chunk, chunk)).astype(BF16)


def _mixer_in_body(x_ref, *refs):
    for r0 in range(0, x_ref.shape[0], ROW_SUB):
        _mixer_in_rows(slice(r0, r0 + ROW_SUB), x_ref, *refs)


def _mixer_in(x, g, w_in, cos, slo, shi, *, tm=512):
    t, d = x.shape
    n_in = w_in.shape[1]
    seq_tiles = SEQ // tm
    row = lambda i: (i, 0)
    pos = lambda i: (i % seq_tiles, 0)
    const = lambda i: (0, 0)
    return pl.pallas_call(
        _mixer_in_body,
        grid=(t // tm,),
        in_specs=[
            pl.BlockSpec((tm, d), row),
            pl.BlockSpec((1, d), const),
            pl.BlockSpec((d, n_in), const),
            pl.BlockSpec((tm, LANES), pos),
            pl.BlockSpec((tm, LANES), pos),
            pl.BlockSpec((tm, LANES), pos),
        ],
        out_specs=[
            pl.BlockSpec((tm, ATTN_WIDTH), row),
            pl.BlockSpec((tm, 4 * LANES), row),
            pl.BlockSpec((tm, SGU_WIDTH), row),
            pl.BlockSpec((tm, SGU_WIDTH), row),
        ],
        out_shape=[
            jax.ShapeDtypeStruct((t, ATTN_WIDTH), BF16),
            jax.ShapeDtypeStruct((t, 4 * LANES), BF16),
            jax.ShapeDtypeStruct((t, SGU_WIDTH), BF16),
            jax.ShapeDtypeStruct((t, SGU_WIDTH), BF16),
        ],
        compiler_params=_params("parallel"),
        name="mixer_in",
    )(x, g, w_in, cos, slo, shi)


def _swa_block(q, kv_prev, kv_cur, sinks_ref, valid):
    lane = lax.broadcasted_iota(jnp.int32, (BLOCK, LANES), 1)
    lo_half = lane < HEAD_DIM
    zero = jnp.zeros((), BF16)
    outs = []
    for g in range(N_KV_HEADS):
        kd = jnp.concatenate([kv_prev[:, g * LANES:(g + 1) * LANES], kv_cur[:, g * LANES:(g + 1) * LANES]], axis=0)
        vd = jnp.concatenate([kv_prev[:, (2 + g) * LANES:(3 + g) * LANES],
                              kv_cur[:, (2 + g) * LANES:(3 + g) * LANES]], axis=0)
        parts = []
        for p in range(Q_PER_KV // 2):
            pair = q[:, (g * Q_PER_KV // 2 + p) * LANES:(g * Q_PER_KV // 2 + p + 1) * LANES]
            parts.append(jnp.where(lo_half, pair, zero))
            parts.append(jnp.where(lo_half, zero, pair))
        qs = jnp.concatenate(parts, axis=0)
        s_all = lax.dot_general(qs, kd, (((1,), (1,)), ((), ())), preferred_element_type=F32)
        es, inv = [], []
        for hq in range(Q_PER_KV):
            sink = sinks_ref[g * Q_PER_KV + hq]
            s = jnp.where(valid, s_all[hq * BLOCK:(hq + 1) * BLOCK], -1e30)
            m = jnp.maximum(jnp.max(s, axis=-1, keepdims=True), sink)
            e = jnp.exp(s - m)
            inv.append(1.0 / (jnp.sum(e, axis=-1, keepdims=True) + jnp.exp(sink - m)))
            es.append(e.astype(BF16))
        o_all = jnp.dot(jnp.concatenate(es, axis=0), vd, preferred_element_type=F32)
        for p in range(Q_PER_KV // 2):
            even = o_all[(2 * p) * BLOCK:(2 * p + 1) * BLOCK] * inv[2 * p]
            odd = o_all[(2 * p + 1) * BLOCK:(2 * p + 2) * BLOCK] * inv[2 * p + 1]
            outs.append(jnp.where(lo_half, even, odd).astype(BF16))
    return jnp.concatenate(outs, axis=1)


def _sgu_tile(u_ref, v_ref, ln_g, ln_b, sw_ref, bt_ref, mix_ref):
    nblk = u_ref.shape[0] // BLOCK
    ri = lax.broadcasted_iota(jnp.int32, (BLOCK, BLOCK), 0)
    ci = lax.broadcasted_iota(jnp.int32, (BLOCK, BLOCK), 1)
    vf = v_ref[...].astype(F32)
    mu = jnp.mean(vf, axis=-1, keepdims=True)
    vc = vf - mu
    var = jnp.mean(vc * vc, axis=-1, keepdims=True)
    vn = (vc * lax.rsqrt(var + EPS) * ln_g + ln_b).astype(BF16)
    for g in range(SGU_GROUPS):
        cols = slice(g * LANES, (g + 1) * LANES)
        w = jnp.where(ci <= ri, sw_ref[g], 0.0).astype(BF16)
        rhs = jnp.concatenate([vn[b * BLOCK:(b + 1) * BLOCK, cols] for b in range(nblk)], axis=1)
        mixed = jnp.dot(w, rhs, preferred_element_type=F32) + bt_ref[:, g:g + 1]
        for b in range(nblk):
            rows = slice(b * BLOCK, (b + 1) * BLOCK)
            mix_ref[rows, ATTN_WIDTH + g * LANES:ATTN_WIDTH + (g + 1) * LANES] = (
                u_ref[rows, cols].astype(F32) * mixed[:, b * LANES:(b + 1) * LANES]).astype(BF16)


def _mixer_out_body(sinks_ref, x_ref, q_ref, kv_ref, kvp_ref, u_ref, v_ref, lng_ref, lnb_ref, sw_ref, sbt_ref,
                    wout_ref, g_ref, o_ref, mix_ref):
    tm = x_ref.shape[0]
    nblk = tm // BLOCK
    blocks_per_seq = SEQ // BLOCK
    first_blk = pl.program_id(0) * nblk
    _sgu_tile(u_ref, v_ref, lng_ref[...], lnb_ref[...], sw_ref, sbt_ref, mix_ref)
    qi = lax.broadcasted_iota(jnp.int32, (BLOCK, 2 * BLOCK), 0)
    kj = lax.broadcasted_iota(jnp.int32, (BLOCK, 2 * BLOCK), 1)
    rel = qi + BLOCK - kj
    band = (rel >= 0) & (rel < BLOCK)
    for b in range(nblk):
        r0 = b * BLOCK
        kv_cur = kv_ref[r0:r0 + BLOCK, :]
        kv_prev = kvp_ref[...] if b == 0 else kv_ref[r0 - BLOCK:r0, :]
        first_key = jnp.where(((first_blk + b) % blocks_per_seq) != 0, 0, BLOCK)
        valid = band & (kj >= first_key)
        mix_ref[r0:r0 + BLOCK, :ATTN_WIDTH] = _swa_block(q_ref[r0:r0 + BLOCK, :], kv_prev, kv_cur, sinks_ref, valid)
    m = jnp.dot(mix_ref[...], wout_ref[...], preferred_element_type=F32)
    o_ref[...] = x_ref[...] + _rms(m, g_ref[...])


def _mixer_out(x, q, kv, u, v, sinks, ln_g, ln_b, sgu_w, sgu_bt, w_out, g, *, tm=512):
    t, d = x.shape
    nblk = tm // BLOCK
    row = lambda i: (i, 0)
    const = lambda i: (0, 0)
    return pl.pallas_call(
        _mixer_out_body,
        grid=(t // tm,),
        in_specs=[
            pl.BlockSpec(memory_space=pltpu.SMEM),
            pl.BlockSpec((tm, d), row),
            pl.BlockSpec((tm, ATTN_WIDTH), row),
            pl.BlockSpec((tm, 4 * LANES), row),
            pl.BlockSpec((BLOCK, 4 * LANES), lambda i: (jnp.maximum(i * nblk - 1, 0), 0)),
            pl.BlockSpec((tm, SGU_WIDTH), row),
            pl.BlockSpec((tm, SGU_WIDTH), row),
            pl.BlockSpec((1, SGU_WIDTH), const),
            pl.BlockSpec((1, SGU_WIDTH), const),
            pl.BlockSpec((SGU_GROUPS, BLOCK, BLOCK), lambda i: (0, 0, 0)),
            pl.BlockSpec((BLOCK, SGU_GROUPS), const),
            pl.BlockSpec((ATTN_WIDTH + SGU_WIDTH, d), const),
            pl.BlockSpec((1, d), const),
        ],
        out_specs=pl.BlockSpec((tm, d), row),
        out_shape=jax.ShapeDtypeStruct((t, d), F32),
        scratch_shapes=[pltpu.VMEM((tm, ATTN_WIDTH + SGU_WIDTH), BF16)],
        compiler_params=_params("parallel"),
        name="mixer_out",
    )(sinks, x, q, kv, kv, u, v, ln_g, ln_b, sgu_w, sgu_bt, w_out, g)


def _pool_body(x_ref, halo_ref, gpre_ref, w_ref, scale_ref, gpost_ref, o_ref):
    tm, d = x_ref.shape
    gd = d // len(POOL_WINDOWS)
    seq_start = (pl.program_id(0) * tm) % SEQ == 0
    x = x_ref[...]
    h = _rms(x, gpre_ref[...])
    h_halo = jnp.where(seq_start, 0.0, _rms(halo_ref[...], gpre_ref[...]))
    ext = jnp.concatenate([h_halo, h], axis=0)
    t_pos = (pl.program_id(0) * tm) % SEQ + lax.broadcasted_iota(jnp.int32, (tm, 1), 0)
    count = (t_pos + 1).astype(F32)
    ys = []
    for gi, w in enumerate(POOL_WINDOWS):
        s = ext[:, gi * gd:(gi + 1) * gd]
        span = 1
        while span < w:
            s = s + pltpu.roll(s, span, 0)
            span *= 2
        mean = s[POOL_HALO:] * (1.0 / jnp.minimum(count, float(w)))
        pooled = (mean - h[:, gi * gd:(gi + 1) * gd]).astype(BF16)
        ys.append(jnp.dot(pooled, w_ref[gi], preferred_element_type=F32))
    y = jnp.concatenate(ys, axis=1) * scale_ref[...]
    o_ref[...] = x + _rms(y, gpost_ref[...])


def _pool(x, g_pre, pool_w, pool_scale, g_post, *, tm=1024):
    t, d = x.shape
    gd = d // len(POOL_WINDOWS)
    halo_blocks = tm // POOL_HALO
    row = lambda i: (i, 0)
    const = lambda i: (0, 0)
    return pl.pallas_call(
        _pool_body,
        grid=(t // tm,),
        in_specs=[
            pl.BlockSpec((tm, d), row),
            pl.BlockSpec((POOL_HALO, d), lambda i: (jnp.maximum(i * halo_blocks - 1, 0), 0)),
            pl.BlockSpec((1, d), const),
            pl.BlockSpec((len(POOL_WINDOWS), gd, gd), lambda i: (0, 0, 0)),
            pl.BlockSpec((1, d), const),
            pl.BlockSpec((1, d), const),
        ],
        out_specs=pl.BlockSpec((tm, d), row),
        out_shape=jax.ShapeDtypeStruct((t, d), F32),
        compiler_params=_params("parallel"),
        name="pool",
    )(x, x, g_pre, pool_w, pool_scale, g_post)


def _mem_kv_body(m_ref, g_ref, wk_ref, wv_ref, k_ref, v_ref):
    h = _rms(m_ref[...], g_ref[...]).astype(BF16)
    k_ref[...] = jnp.dot(h, wk_ref[...], preferred_element_type=F32).astype(BF16)
    v_ref[...] = jnp.dot(h, wv_ref[...], preferred_element_type=F32).astype(BF16)


def _mem_kv(mem, g, wk, wv, *, tm=256):
    t, d = mem.shape
    n = wk.shape[1]
    row = lambda i: (i, 0)
    const = lambda i: (0, 0)
    return pl.pallas_call(
        _mem_kv_body,
        grid=(t // tm,),
        in_specs=[pl.BlockSpec((tm, d), row), pl.BlockSpec((1, d), const),
                  pl.BlockSpec((d, n), const), pl.BlockSpec((d, n), const)],
        out_specs=[pl.BlockSpec((tm, n), row), pl.BlockSpec((tm, n), row)],
        out_shape=[jax.ShapeDtypeStruct((t, n), BF16), jax.ShapeDtypeStruct((t, n), BF16)],
        compiler_params=_params("parallel"),
        name="mem_kv",
    )(mem, g, wk, wv)


def _cross_body(x_ref, gpre_ref, wq_ref, k_ref, v_ref, wo_ref, gpost_ref, o_ref):
    x = x_ref[...]
    h = _rms(x, gpre_ref[...]).astype(BF16)
    q = jnp.dot(h, wq_ref[...], preferred_element_type=F32).astype(BF16)
    outs = []
    for hh in range(X_HEADS):
        sl = slice(hh * X_HEAD_DIM, (hh + 1) * X_HEAD_DIM)
        s = lax.dot_general(q[:, sl], k_ref[:, sl], (((1,), (1,)), ((), ())),
                            preferred_element_type=F32) * (X_HEAD_DIM ** -0.5)
        e = jnp.exp(s - jnp.max(s, axis=-1, keepdims=True))
        denom = jnp.sum(e, axis=-1, keepdims=True)
        o = jnp.dot(e.astype(BF16), v_ref[:, sl], preferred_element_type=F32) * (1.0 / denom)
        outs.append(o.astype(BF16))
    y = jnp.dot(jnp.concatenate(outs, axis=1), wo_ref[...], preferred_element_type=F32)
    o_ref[...] = x + _rms(y, gpost_ref[...])


def _cross(x, g_pre, wq, k, v, wo, g_post, *, tm=1024):
    t, d = x.shape
    n = wq.shape[1]
    mem_len = k.shape[0] // (t // SEQ)
    tiles_per_seq = SEQ // tm
    row = lambda i: (i, 0)
    const = lambda i: (0, 0)
    batch = lambda i: (i // tiles_per_seq, 0)
    return pl.pallas_call(
        _cross_body,
        grid=(t // tm,),
        in_specs=[
            pl.BlockSpec((tm, d), row),
            pl.BlockSpec((1, d), const),
            pl.BlockSpec((d, n), const),
            pl.BlockSpec((mem_len, n), batch),
            pl.BlockSpec((mem_len, n), batch),
            pl.BlockSpec((n, d), const),
            pl.BlockSpec((1, d), const),
        ],
        out_specs=pl.BlockSpec((tm, d), row),
        out_shape=jax.ShapeDtypeStruct((t, d), F32),
        compiler_params=_params("parallel"),
        name="cross",
    )(x, g_pre, wq, k, v, wo, g_post)


def _rope_tables():
    half = ROPE_DIM // 2
    dim = jnp.arange(LANES) % HEAD_DIM
    inv = ROPE_THETA ** (-(dim % half).astype(F32) * 2.0 / ROPE_DIM)
    ang = jnp.arange(SEQ, dtype=F32)[:, None] * inv[None, :]
    cos, sin = jnp.cos(ang), jnp.sin(ang)
    cos_t = jnp.where(dim < ROPE_DIM, cos, 1.0)
    sin_lo = jnp.where(dim < half, -sin, 0.0)
    sin_hi = jnp.where((dim >= half) & (dim < ROPE_DIM), sin, 0.0)
    return cos_t, sin_lo, sin_hi


def kernel(x, mem, norms, mem_norm, ffn1_wg, ffn1_wu, ffn1_wd, ffn2_wg, ffn2_wu, ffn2_wd, x_wq, x_wk, x_wv, x_wo,
           mix_w_in, mix_w_out, attn_sinks, sgu_ln_g, sgu_ln_b, sgu_w, sgu_b, pool_w, pool_scale):
    b, s, d = x.shape
    depth = norms.shape[0]
    assert s == SEQ
    bf = lambda a: a.astype(BF16)
    xt = x.reshape(b * s, d)
    memt = mem.reshape(b * mem.shape[1], d)
    cos, slo, shi = _rope_tables()
    ffn1 = (ffn1_wg, ffn1_wu, ffn1_wd)
    ffn2 = (ffn2_wg, ffn2_wu, ffn2_wd)
    w_next = [bf(w[0]) for w in ffn1]
    for layer in range(depth):
        g = norms[layer][:, None, :]
        xt, w_next = _ffn(xt, g[0], g[1], *w_next, ffn2, layer)
        i = layer // 2
        if layer % 2 == 0:
            q, kv, u, v = _mixer_in(xt, g[2], bf(mix_w_in[i]), cos, slo, shi)
            xt = _mixer_out(xt, q, kv, u, v, attn_sinks[i], sgu_ln_g[i][None], sgu_ln_b[i][None], sgu_w[i],
                            sgu_b[i].T, bf(mix_w_out[i]), g[3])
        else:
            xt = _pool(xt, g[2], bf(pool_w[i]), pool_scale[i][None], g[3])
        kx, vx = _mem_kv(memt, mem_norm[layer][None], bf(x_wk[layer]), bf(x_wv[layer]))
        xt = _cross(xt, g[4], bf(x_wq[layer]), kx, vx, bf(x_wo[layer]), g[5])
        more = layer + 1 < depth
        xt, w_next = _ffn(xt, g[6], g[7], *w_next, ffn1 if more else (), layer + 1 if more else 0)
    return xt.reshape(b, s, d)
```
